```python
import jax
import jax.numpy as jnp
from jax import lax
import numpy as np

D_MODEL = 2048
BATCH = 16
SEQ = 2048
DEPTH = 4

CONV_WIDTH = 512
CONV_K = 31
POOL_WIDTH = 512
POOL_WINDOWS = (2, 4, 8, 16)
POOL_GROUP = POOL_WIDTH // len(POOL_WINDOWS)
ATT_HEADS = 8
HEAD_DIM = 128
ATT_WIDTH = ATT_HEADS * HEAD_DIM
N_BRANCH = 3
MIX_WIDTH = CONV_WIDTH + POOL_WIDTH + ATT_WIDTH
Q_BLOCK = 128
D_FF = 5632
N_EXPERTS = 8
TOP_K = 2
N_DENSE = (DEPTH + 1) // 2
N_MOE = DEPTH // 2
EPS = 1e-6

OFF_B = 2 * CONV_WIDTH
OFF_Q = OFF_B + POOL_WIDTH
OFF_K = OFF_Q + ATT_WIDTH
OFF_V = OFF_K + ATT_WIDTH
OFF_F = OFF_V + ATT_WIDTH
OFF_G = OFF_F + ATT_HEADS
N_IN = OFF_G + N_BRANCH * D_MODEL

kernel_name = 'hybrid_conv_pool_fox_moe'


def rmsnorm(x, g):
    xf = x.astype(jnp.float32)
    y = xf * lax.rsqrt(jnp.mean(xf * xf, axis=-1, keepdims=True) + EPS)
    return (y * g.astype(jnp.float32)).astype(x.dtype)


def layernorm(x, g, b):
    xf = x.astype(jnp.float32)
    mu = jnp.mean(xf, axis=-1, keepdims=True)
    xc = xf - mu
    y = xc * lax.rsqrt(jnp.mean(xc * xc, axis=-1, keepdims=True) + EPS)
    return (y * g.astype(jnp.float32) + b.astype(jnp.float32)).astype(x.dtype)


def conv_mixer(u, conv_w, conv_b, cln_g, cln_b):
    a, gt = u[..., :CONV_WIDTH], u[..., CONV_WIDTH:]
    z = a * jax.nn.sigmoid(gt)
    zp = jnp.pad(z, ((0, 0), (CONV_K - 1, 0), (0, 0)))
    z = lax.conv_general_dilated(zp, conv_w[:, None, :], (1,), 'VALID',
                                 dimension_numbers=('NWC', 'WIO', 'NWC'),
                                 feature_group_count=CONV_WIDTH) + conv_b
    z = layernorm(z, cln_g, cln_b)
    return jax.nn.silu(z)


def pool_mixer(u, pool_w, pool_scale):
    B, S, _ = u.shape
    uf = u.astype(jnp.float32)
    cs0 = jnp.pad(jnp.cumsum(uf, axis=1), ((0, 0), (1, 0), (0, 0)))
    hi = cs0[:, 1:]
    pos = jnp.arange(1, S + 1, dtype=jnp.float32)[:, None]
    outs = []
    for gi, w in enumerate(POOL_WINDOWS):
        sl = slice(gi * POOL_GROUP, (gi + 1) * POOL_GROUP)
        lo = jnp.pad(cs0[:, :S + 1 - w, sl], ((0, 0), (w - 1, 0), (0, 0)))
        outs.append((hi[..., sl] - lo) / jnp.minimum(pos, w))
    p = (jnp.concatenate(outs, axis=-1) - uf).astype(u.dtype)
    p = p.reshape(B, S, len(POOL_WINDOWS), POOL_GROUP)
    y = jnp.einsum('bsgc,gcd->bsgd', p, pool_w).reshape(B, S, POOL_WIDTH)
    return y * pool_scale


def forgetting_attention(q, k, v, f_logit, q_g, k_g):
    B, S, H, Dh = q.shape
    q = rmsnorm(q, q_g).transpose(0, 2, 1, 3)
    k = rmsnorm(k, k_g).transpose(0, 2, 1, 3)
    v = v.transpose(0, 2, 1, 3)
    c = jnp.cumsum(jax.nn.log_sigmoid(f_logit.astype(jnp.float32)), axis=1).transpose(0, 2, 1)
    scale = HEAD_DIM ** -0.5
    outs = []
    for i in range(S // Q_BLOCK):
        q0, q1 = i * Q_BLOCK, (i + 1) * Q_BLOCK
        s = jnp.einsum('bhqd,bhkd->bhqk', q[:, :, q0:q1], k[:, :, :q1],
                       preferred_element_type=jnp.float32) * scale
        s = s + c[:, :, q0:q1, None] - c[:, :, None, :q1]
        mask = jnp.arange(q0, q1)[:, None] >= jnp.arange(q1)[None, :]
        s = jnp.where(mask, s, -jnp.inf)
        p = jax.nn.softmax(s, axis=-1).astype(v.dtype)
        outs.append(jnp.einsum('bhqk,bhkd->bqhd', p, v[:, :, :q1]))
    return jnp.concatenate(outs, axis=1).reshape(B, S, H * Dh)


def mixing_sublayer(x, norm_g, w_in, b_f, b_gate, conv_w, conv_b, cln_g, cln_b,
                    pool_w, pool_scale, q_g, k_g, w_br, w_o):
    B, S, D = x.shape
    h = rmsnorm(x, norm_g)
    z = h @ w_in
    y_a = conv_mixer(z[..., :OFF_B], conv_w, conv_b, cln_g, cln_b)
    y_b = pool_mixer(z[..., OFF_B:OFF_Q], pool_w, pool_scale)
    q = z[..., OFF_Q:OFF_K].reshape(B, S, ATT_HEADS, HEAD_DIM)
    k = z[..., OFF_K:OFF_V].reshape(B, S, ATT_HEADS, HEAD_DIM)
    v = z[..., OFF_V:OFF_F].reshape(B, S, ATT_HEADS, HEAD_DIM)
    y_c = forgetting_attention(q, k, v, z[..., OFF_F:OFF_G] + b_f, q_g, k_g)
    gates = jax.nn.sigmoid((z[..., OFF_G:] + b_gate).astype(jnp.float32)).astype(x.dtype)
    gates = gates.reshape(B, S, N_BRANCH, D)
    merged = (gates[:, :, 0] * (y_a @ w_br[:CONV_WIDTH])
              + gates[:, :, 1] * (y_b @ w_br[CONV_WIDTH:CONV_WIDTH + POOL_WIDTH])
              + gates[:, :, 2] * (y_c @ w_br[CONV_WIDTH + POOL_WIDTH:]))
    return x + merged @ w_o


def swiglu(h, wg, wu, wd):
    return (jax.nn.silu(h @ wg) * (h @ wu)) @ wd


def moe_swiglu(h, router, wg, wu, wd):
    logits = (h @ router).astype(jnp.float32)
    vals, idx = lax.top_k(logits, TOP_K)
    w = jax.nn.softmax(vals, axis=-1)
    dense_w = jnp.sum(jax.nn.one_hot(idx, N_EXPERTS, dtype=jnp.float32) * w[..., None], axis=-2)
    dense_w = dense_w.astype(h.dtype)
    y = jnp.zeros_like(h)
    for e in range(N_EXPERTS):
        y = y + dense_w[..., e:e + 1] * swiglu(h, wg[e], wu[e], wd[e])
    return y


def setup_inputs(seed: int = 0) -> dict:
    key = jax.random.key(seed)
    ks = iter(jax.random.split(key, 40))
    f32 = jnp.float32

    def nrm(shape, fan_in):
        return jax.random.normal(next(ks), shape, f32) * (fan_in ** -0.5)

    def gain(shape):
        return 1.0 + 0.05 * jax.random.normal(next(ks), shape, f32)

    def small(shape):
        return 0.02 * jax.random.normal(next(ks), shape, f32)

    x = jax.random.normal(next(ks), (BATCH, SEQ, D_MODEL), f32)
    w_br = jnp.concatenate([nrm((DEPTH, CONV_WIDTH, D_MODEL), CONV_WIDTH),
                            nrm((DEPTH, POOL_WIDTH, D_MODEL), POOL_WIDTH),
                            nrm((DEPTH, ATT_WIDTH, D_MODEL), ATT_WIDTH)], axis=1)
    return {
        'x': x,
        'norm1_g': gain((DEPTH, D_MODEL)),
        'w_in': nrm((DEPTH, D_MODEL, N_IN), D_MODEL),
        'b_f': jax.random.uniform(next(ks), (DEPTH, ATT_HEADS), f32, 1.0, 6.0),
        'b_gate': small((DEPTH, N_BRANCH * D_MODEL)),
        'conv_w': nrm((DEPTH, CONV_K, CONV_WIDTH), CONV_K),
        'conv_b': small((DEPTH, CONV_WIDTH)),
        'cln_g': gain((DEPTH, CONV_WIDTH)),
        'cln_b': small((DEPTH, CONV_WIDTH)),
        'pool_w': nrm((DEPTH, len(POOL_WINDOWS), POOL_GROUP, POOL_GROUP), POOL_GROUP),
        'pool_scale': gain((DEPTH, POOL_WIDTH)),
        'q_g': gain((DEPTH, HEAD_DIM)),
        'k_g': gain((DEPTH, HEAD_DIM)),
        'w_br': w_br,
        'w_o': nrm((DEPTH, D_MODEL, D_MODEL), D_MODEL),
        'norm2_g': gain((DEPTH, D_MODEL)),
        'ffn_wg': nrm((N_DENSE, D_MODEL, D_FF), D_MODEL),
        'ffn_wu': nrm((N_DENSE, D_MODEL, D_FF), D_MODEL),
        'ffn_wd': nrm((N_DENSE, D_FF, D_MODEL), D_FF),
        'router': nrm((N_MOE, D_MODEL, N_EXPERTS), D_MODEL),
        'exp_wg': nrm((N_MOE, N_EXPERTS, D_MODEL, D_FF), D_MODEL),
        'exp_wu': nrm((N_MOE, N_EXPERTS, D_MODEL, D_FF), D_MODEL),
        'exp_wd': nrm((N_MOE, N_EXPERTS, D_FF, D_MODEL), D_FF),
    }


def reference(x, norm1_g, w_in, b_f, b_gate, conv_w, conv_b, cln_g, cln_b,
              pool_w, pool_scale, q_g, k_g, w_br, w_o, norm2_g,
              ffn_wg, ffn_wu, ffn_wd, router, exp_wg, exp_wu, exp_wd):
    for l in range(DEPTH):
        x = mixing_sublayer(x, norm1_g[l], w_in[l], b_f[l], b_gate[l], conv_w[l], conv_b[l],
                            cln_g[l], cln_b[l], pool_w[l], pool_scale[l], q_g[l], k_g[l],
                            w_br[l], w_o[l])
        h = rmsnorm(x, norm2_g[l])
        if l % 2 == 0:
            j = l // 2
            x = x + swiglu(h, ffn_wg[j], ffn_wu[j], ffn_wd[j])
        else:
            j = l // 2
            x = x + moe_swiglu(h, router[j], exp_wg[j], exp_wu[j], exp_wd[j])
    return x
```

```python
import functools

import jax
import jax.numpy as jnp
from jax import lax
from jax.experimental import pallas as pl
from jax.experimental.pallas import tpu as pltpu

EPS = 1e-6
CONV_WIDTH = 512
CONV_K = 31
POOL_WIDTH = 512
POOL_WINDOWS = (2, 4, 8, 16)
POOL_GROUP = POOL_WIDTH // len(POOL_WINDOWS)
ATT_HEADS = 8
HEAD_DIM = 128
ATT_WIDTH = ATT_HEADS * HEAD_DIM
N_BRANCH = 3
N_EXPERTS = 8

OFF_B = 2 * CONV_WIDTH
OFF_Q = OFF_B + POOL_WIDTH
OFF_K = OFF_Q + ATT_WIDTH
OFF_V = OFF_K + ATT_WIDTH
OFF_F = OFF_V + ATT_WIDTH
OFF_G = OFF_F + ATT_HEADS

LANES = 128
SUBLANES = 8
HALO = 32
NEG_BIG = -1e30
VMEM_LIMIT_BYTES = 56 * 1024 * 1024

F32 = jnp.float32
BF16 = jnp.bfloat16


def _tile(n, pref):
    t = min(n, pref)
    while n % t:
        t -= 1
    return t


def _params(*sem):
    return pltpu.CompilerParams(dimension_semantics=sem, vmem_limit_bytes=VMEM_LIMIT_BYTES)


def _rmsnorm_rows(x, g):
    ms = jnp.mean(x * x, axis=-1, keepdims=True)
    return x * lax.rsqrt(ms + EPS) * g


def _sigmoid(x):
    return 1.0 / (1.0 + jnp.exp(-x))


def _log_sigmoid(x):
    return jnp.minimum(x, 0.0) - jnp.log1p(jnp.exp(-jnp.abs(x)))


def _top2_dense(logits):
    lane = lax.broadcasted_iota(jnp.int32, logits.shape, 1).astype(F32)
    lg = jnp.where(lane < N_EXPERTS, logits, NEG_BIG)
    m1 = jnp.max(lg, axis=-1, keepdims=True)
    i1 = jnp.min(jnp.where(lg == m1, lane, float(LANES)), axis=-1, keepdims=True)
    lg2 = jnp.where(lane == i1, NEG_BIG, lg)
    m2 = jnp.max(lg2, axis=-1, keepdims=True)
    i2 = jnp.min(jnp.where(lg2 == m2, lane, float(LANES)), axis=-1, keepdims=True)
    e = jnp.exp(m2 - m1)
    w1 = 1.0 / (1.0 + e)
    w2 = e / (1.0 + e)
    return jnp.where(lane == i1, w1, jnp.where(lane == i2, w2, 0.0))


def _norm_proj_kernel(x_ref, g_ref, w_ref, b_ref, o_ref, h_ref, *, act, precise):
    @pl.when(pl.program_id(1) == 0)
    def _():
        h_ref[...] = _rmsnorm_rows(x_ref[...], g_ref[...]).astype(h_ref.dtype)

    y = jnp.dot(h_ref[...], w_ref[...], preferred_element_type=F32,
                precision=lax.Precision.HIGHEST if precise else None)
    y = y + b_ref[...]
    if act == "sigmoid":
        y = _sigmoid(y)
    elif act == "top2":
        y = _top2_dense(y)
    o_ref[...] = y.astype(o_ref.dtype)


def _norm_proj(x2, g, w, b, *, act=None, precise=False, out_dtype, tm, tn):
    m, d = x2.shape
    n = w.shape[1]
    tm, tn = _tile(m, tm), _tile(n, tn)
    return pl.pallas_call(
        functools.partial(_norm_proj_kernel, act=act, precise=precise),
        grid=(m // tm, n // tn),
        in_specs=[pl.BlockSpec((tm, d), lambda i, j: (i, 0)),
                  pl.BlockSpec((1, d), lambda i, j: (0, 0)),
                  pl.BlockSpec((d, tn), lambda i, j: (0, j)),
                  pl.BlockSpec((1, tn), lambda i, j: (0, j))],
        out_specs=pl.BlockSpec((tm, tn), lambda i, j: (i, j)),
        out_shape=jax.ShapeDtypeStruct((m, n), out_dtype),
        scratch_shapes=[pltpu.VMEM((tm, d), F32 if precise else BF16)],
        compiler_params=_params("parallel", "arbitrary"),
        name="norm_proj_" + (act or "plain"),
    )(x2, g, w, b)


def _cumsum_kernel(f_ref, o_ref, *, chunk):
    s = f_ref.shape[0]
    row = lax.broadcasted_iota(jnp.int32, (chunk, chunk), 0)
    col = lax.broadcasted_iota(jnp.int32, (chunk, chunk), 1)
    tril = (row >= col).astype(F32)
    carry = jnp.zeros((1, f_ref.shape[1]), F32)
    for c in range(s // chunk):
        ls = _log_sigmoid(f_ref[c * chunk:(c + 1) * chunk, :])
        cs = jnp.dot(tril, ls, preferred_element_type=F32, precision=lax.Precision.HIGHEST) + carry
        o_ref[c * chunk:(c + 1) * chunk, :] = cs
        carry = cs[chunk - 1:chunk, :]


def _log_forget_cumsum(f3):
    b, s, n = f3.shape
    chunk = _tile(s, LANES)
    return pl.pallas_call(
        functools.partial(_cumsum_kernel, chunk=chunk),
        grid=(b,),
        in_specs=[pl.BlockSpec((None, s, n), lambda i: (i, 0, 0))],
        out_specs=pl.BlockSpec((None, s, n), lambda i: (i, 0, 0)),
        out_shape=jax.ShapeDtypeStruct((b, s, n), F32),
        compiler_params=_params("parallel"),
        name="log_forget_cumsum",
    )(f3)


def _shift_rows(win, p):
    if p == 0:
        return win
    return pltpu.roll(win, win.shape[0] - p, axis=0)


def _mixers_kernel(z_ref, cw_ref, cb_ref, lg_ref, lb_ref, pw_ref, ps_ref, o_ref, zp_ref, up_ref, cv_ref, *, rows):
    s = z_ref.shape[0]
    n_chunks = s // rows
    zp_ref[0:HALO, :] = jnp.zeros((HALO, CONV_WIDTH), F32)
    up_ref[0:HALO, :] = jnp.zeros((HALO, POOL_WIDTH), F32)

    def fill(c, _):
        r0 = pl.multiple_of(c * rows, rows)
        a = z_ref[pl.ds(r0, rows), 0:CONV_WIDTH].astype(F32)
        gt = z_ref[pl.ds(r0, rows), CONV_WIDTH:OFF_B].astype(F32)
        zp_ref[pl.ds(r0 + HALO, rows), :] = a * _sigmoid(gt)
        up_ref[pl.ds(r0 + HALO, rows), :] = z_ref[pl.ds(r0, rows), OFF_B:OFF_Q].astype(F32)
        return 0

    lax.fori_loop(0, n_chunks, fill, 0)

    lead = HALO - (CONV_K - 1)

    def mix(c, _):
        r0 = pl.multiple_of(c * rows, rows)
        for lg in range(CONV_WIDTH // LANES):
            ls = slice(lg * LANES, (lg + 1) * LANES)
            win = zp_ref[pl.ds(r0, rows + HALO), ls]
            acc = jnp.zeros((rows, LANES), F32) + cb_ref[:, ls]
            for p in range(SUBLANES):
                wp = _shift_rows(win, p)
                for k in range(CONV_K):
                    if (lead + k) % SUBLANES == p:
                        off = lead + k - p
                        acc = acc + cw_ref[k:k + 1, ls] * wp[off:off + rows, :]
            cv_ref[:, ls] = acc
        y = cv_ref[...]
        mu = jnp.mean(y, axis=-1, keepdims=True)
        yc = y - mu
        var = jnp.mean(yc * yc, axis=-1, keepdims=True)
        y = yc * lax.rsqrt(var + EPS) * lg_ref[...] + lb_ref[...]
        o_ref[pl.ds(r0, rows), 0:CONV_WIDTH] = (y * _sigmoid(y)).astype(o_ref.dtype)

        half = HALO // 2
        pos = (lax.broadcasted_iota(jnp.int32, (rows, LANES), 0) + (r0 + 1)).astype(F32)
        for gi, w in enumerate(POOL_WINDOWS):
            ls = slice(gi * POOL_GROUP, (gi + 1) * POOL_GROUP)
            win = up_ref[pl.ds(r0 + half, rows + half), ls]
            tot = win
            span = 1
            while span < w:
                tot = tot + pltpu.roll(tot, span, axis=0)
                span *= 2
            u = win[half:half + rows, :]
            p = tot[half:half + rows, :] / jnp.minimum(pos, float(w)) - u
            yb = jnp.dot(p.astype(BF16), pw_ref[gi], preferred_element_type=F32) * ps_ref[:, ls]
            o_ref[pl.ds(r0, rows), CONV_WIDTH + gi * POOL_GROUP:CONV_WIDTH + (gi + 1) * POOL_GROUP] = yb.astype(o_ref.dtype)
        return 0

    lax.fori_loop(0, n_chunks, mix, 0)


def _mixers(z3, conv_w, conv_b, cln_g, cln_b, pool_w, pool_scale):
    b, s, n = z3.shape
    rows = _tile(s, 256)
    width = OFF_Q
    assert n % width == 0
    return pl.pallas_call(
        functools.partial(_mixers_kernel, rows=rows),
        grid=(b,),
        in_specs=[pl.BlockSpec((None, s, width), lambda i: (i, 0, 0)),
                  pl.BlockSpec(conv_w.shape, lambda i: (0, 0)),
                  pl.BlockSpec(conv_b.shape, lambda i: (0, 0)),
                  pl.BlockSpec(cln_g.shape, lambda i: (0, 0)),
                  pl.BlockSpec(cln_b.shape, lambda i: (0, 0)),
                  pl.BlockSpec(pool_w.shape, lambda i: (0, 0, 0)),
                  pl.BlockSpec(pool_scale.shape, lambda i: (0, 0))],
        out_specs=pl.BlockSpec((None, s, CONV_WIDTH + POOL_WIDTH), lambda i: (i, 0, 0)),
        out_shape=jax.ShapeDtypeStruct((b, s, CONV_WIDTH + POOL_WIDTH), BF16),
        scratch_shapes=[pltpu.VMEM((s + HALO, CONV_WIDTH), F32),
                        pltpu.VMEM((s + HALO, POOL_WIDTH), F32),
                        pltpu.VMEM((rows, CONV_WIDTH), F32)],
        compiler_params=_params("parallel"),
        name="conv_pool_mixers",
    )(z3, conv_w, conv_b, cln_g, cln_b, pool_w, pool_scale)


def _split3(c):
    hi = c.astype(BF16).astype(F32)
    r1 = c - hi
    mid = r1.astype(BF16).astype(F32)
    return hi, mid, r1 - mid


def _qk_prep_kernel(qa_ref, qb_ref, ka_ref, kb_ref, c_ref, qg_ref, kg_ref, qo_ref, ko_ref):
    ts = c_ref.shape[0]
    scale = HEAD_DIM ** -0.5
    hi, mid, lo = _split3(c_ref[...])
    lane = lax.broadcasted_iota(jnp.int32, (ts, LANES), 1)
    half = ATT_HEADS // 2
    for h in range(ATT_HEADS):
        q_src, k_src = (qa_ref, ka_ref) if h < half else (qb_ref, kb_ref)
        ls = slice((h % half) * HEAD_DIM, (h % half + 1) * HEAD_DIM)
        qn = _rmsnorm_rows(q_src[:, ls].astype(F32), qg_ref[...]) * scale
        kn = _rmsnorm_rows(k_src[:, ls].astype(F32), kg_ref[...])
        ch, cm, cl = hi[:, h:h + 1], mid[:, h:h + 1], lo[:, h:h + 1]
        ones = jnp.where(lane < 6, 1.0, 0.0)
        q_ext = jnp.where(lane == 0, ch, jnp.where(lane == 1, cm, jnp.where(lane == 2, cl, ones)))
        k_ext = jnp.where(lane == 3, -ch, jnp.where(lane == 4, -cm, jnp.where(lane == 5, -cl, ones)))
        qo_ref[h, :, 0:HEAD_DIM] = qn.astype(BF16)
        qo_ref[h, :, HEAD_DIM:2 * HEAD_DIM] = q_ext.astype(BF16)
        ko_ref[h, :, 0:HEAD_DIM] = kn.astype(BF16)
        ko_ref[h, :, HEAD_DIM:2 * HEAD_DIM] = k_ext.astype(BF16)


def _qk_prep(z3, c3, q_g, k_g):
    b, s, _ = z3.shape
    ts = _tile(s, 512)
    blk = (ATT_HEADS // 2) * HEAD_DIM
    qb0, kb0 = OFF_Q // blk, OFF_K // blk
    assert OFF_Q % blk == 0 and OFF_K % blk == 0
    out = jax.ShapeDtypeStruct((b, ATT_HEADS, s, 2 * HEAD_DIM), BF16)
    zspec = lambda cb: pl.BlockSpec((None, ts, blk), lambda i, j, cb=cb: (i, j, cb))
    ospec = pl.BlockSpec((None, ATT_HEADS, ts, 2 * HEAD_DIM), lambda i, j: (i, 0, j, 0))
    return pl.pallas_call(
        _qk_prep_kernel,
        grid=(b, s // ts),
        in_specs=[zspec(qb0), zspec(qb0 + 1), zspec(kb0), zspec(kb0 + 1),
                  pl.BlockSpec((None, ts, LANES), lambda i, j: (i, j, 0)),
                  pl.BlockSpec((1, HEAD_DIM), lambda i, j: (0, 0)),
                  pl.BlockSpec((1, HEAD_DIM), lambda i, j: (0, 0))],
        out_specs=[ospec, ospec],
        out_shape=[out, out],
        compiler_params=_params("parallel", "parallel"),
        name="qk_prep",
    )(z3, z3, z3, z3, c3, q_g, k_g)


def _attn_kernel(q_ref, k_ref, v_ref, o_ref, m_ref, l_ref, acc_ref, *, tq):
    i = pl.program_id(2)
    q = q_ref[...]
    m_ref[...] = jnp.full(m_ref.shape, NEG_BIG, F32)
    l_ref[...] = jnp.zeros(l_ref.shape, F32)
    acc_ref[...] = jnp.zeros(acc_ref.shape, F32)

    def block(j, masked):
        r0 = pl.multiple_of(j * tq, tq)
        k = k_ref[pl.ds(r0, tq), :]
        v = v_ref[pl.ds(r0, tq), :]
        s = lax.dot_general(q, k, (((1,), (1,)), ((), ())), preferred_element_type=F32)
        if masked:
            row = lax.broadcasted_iota(jnp.int32, s.shape, 0)
            col = lax.broadcasted_iota(jnp.int32, s.shape, 1)
            s = jnp.where(row >= col, s, NEG_BIG)
        m_prev = m_ref[...]
        m_new = jnp.maximum(m_prev, jnp.max(s, axis=-1, keepdims=True))
        alpha = jnp.exp(m_prev - m_new)
        p = jnp.exp(s - m_new)
        l_ref[...] = alpha * l_ref[...] + jnp.sum(p, axis=-1, keepdims=True)
        acc_ref[...] = alpha * acc_ref[...] + jnp.dot(p.astype(BF16), v, preferred_element_type=F32)
        m_ref[...] = m_new

    def body(j, _):
        block(j, False)
        return 0

    lax.fori_loop(0, i, body, 0)
    block(i, True)
    o_ref[...] = (acc_ref[...] / l_ref[...]).astype(o_ref.dtype)


def _attention(qp, kp, z3):
    b, h, s, dk = qp.shape
    tq = _tile(s, 512)
    v0 = OFF_V // HEAD_DIM
    return pl.pallas_call(
        functools.partial(_attn_kernel, tq=tq),
        grid=(b, h, s // tq),
        in_specs=[pl.BlockSpec((None, None, tq, dk), lambda bi, hi, i: (bi, hi, i, 0)),
                  pl.BlockSpec((None, None, s, dk), lambda bi, hi, i: (bi, hi, 0, 0)),
                  pl.BlockSpec((None, s, HEAD_DIM), lambda bi, hi, i: (bi, 0, v0 + hi))],
        out_specs=pl.BlockSpec((None, tq, HEAD_DIM), lambda bi, hi, i: (bi, i, hi)),
        out_shape=jax.ShapeDtypeStruct((b, s, ATT_WIDTH), BF16),
        scratch_shapes=[pltpu.VMEM((tq, 1), F32), pltpu.VMEM((tq, 1), F32), pltpu.VMEM((tq, HEAD_DIM), F32)],
        compiler_params=_params("parallel", "parallel", "arbitrary"),
        name="fox_attention",
    )(qp, kp, z3)


def _merge_kernel(ya_ref, yb_ref, yc_ref, wa_ref, wb_ref, wc_ref, g0_ref, g1_ref, g2_ref, o_ref):
    a = jnp.dot(ya_ref[...], wa_ref[...], preferred_element_type=F32)
    b = jnp.dot(yb_ref[...], wb_ref[...], preferred_element_type=F32)
    c = jnp.dot(yc_ref[...], wc_ref[...], preferred_element_type=F32)
    merged = g0_ref[...].astype(F32) * a + g1_ref[...].astype(F32) * b + g2_ref[...].astype(F32) * c
    o_ref[...] = merged.astype(o_ref.dtype)


def _merge(y_ab, y_c, w_br, gates, *, tm=1024, tn=512):
    m = y_ab.shape[0]
    d = w_br.shape[1]
    tm, tn = _tile(m, tm), _tile(d, tn)
    nj = d // tn
    return pl.pallas_call(
        _merge_kernel,
        grid=(m // tm, nj),
        in_specs=[pl.BlockSpec((tm, CONV_WIDTH), lambda i, j: (i, 0)),
                  pl.BlockSpec((tm, POOL_WIDTH), lambda i, j: (i, 1)),
                  pl.BlockSpec((tm, ATT_WIDTH), lambda i, j: (i, 0)),
                  pl.BlockSpec((CONV_WIDTH, tn), lambda i, j: (0, j)),
                  pl.BlockSpec((POOL_WIDTH, tn), lambda i, j: (1, j)),
                  pl.BlockSpec((ATT_WIDTH, tn), lambda i, j: (1, j)),
                  pl.BlockSpec((tm, tn), lambda i, j: (i, j)),
                  pl.BlockSpec((tm, tn), lambda i, j: (i, nj + j)),
                  pl.BlockSpec((tm, tn), lambda i, j: (i, 2 * nj + j))],
        out_specs=pl.BlockSpec((tm, tn), lambda i, j: (i, j)),
        out_shape=jax.ShapeDtypeStruct((m, d), BF16),
        compiler_params=_params("parallel", "arbitrary"),
        name="branch_merge",
    )(y_ab, y_ab, y_c, w_br, w_br, w_br, gates, gates, gates)


def _out_proj_kernel(a_ref, w_ref, x_ref, o_ref):
    o_ref[...] = x_ref[...] + jnp.dot(a_ref[...], w_ref[...], preferred_element_type=F32)


def _out_proj(a, w, x2, *, tm=1024, tn=512):
    m, k = a.shape
    n = w.shape[1]
    tm, tn = _tile(m, tm), _tile(n, tn)
    return pl.pallas_call(
        _out_proj_kernel,
        grid=(m // tm, n // tn),
        in_specs=[pl.BlockSpec((tm, k), lambda i, j: (i, 0)),
                  pl.BlockSpec((k, tn), lambda i, j: (0, j)),
                  pl.BlockSpec((tm, tn), lambda i, j: (i, j))],
        out_specs=pl.BlockSpec((tm, tn), lambda i, j: (i, j)),
        out_shape=jax.ShapeDtypeStruct((m, n), F32),
        compiler_params=_params("parallel", "arbitrary"),
        name="out_proj_residual",
    )(a, w, x2)


def _swiglu_partial(h, wg, wu, wd):
    a = jnp.dot(h, wg, preferred_element_type=F32)
    u = jnp.dot(h, wu, preferred_element_type=F32)
    act = (a * _sigmoid(a) * u).astype(BF16)
    return jnp.dot(act, wd, preferred_element_type=F32)


def _ffn_kernel(x_ref, g_ref, wg_ref, wu_ref, wd_ref, o_ref, h_ref, acc_ref):
    f = pl.program_id(1)

    @pl.when(f == 0)
    def _():
        h_ref[...] = _rmsnorm_rows(x_ref[...], g_ref[...]).astype(BF16)
        acc_ref[...] = jnp.zeros(acc_ref.shape, F32)

    acc_ref[...] += _swiglu_partial(h_ref[...], wg_ref[...], wu_ref[...], wd_ref[...])

    @pl.when(f == pl.num_programs(1) - 1)
    def _():
        o_ref[...] = x_ref[...] + acc_ref[...]


def _ffn(x2, g, wg, wu, wd, *, tm=512, tf=512):
    m, d = x2.shape
    ff = wg.shape[1]
    tm, tf = _tile(m, tm), _tile(ff, tf)
    return pl.pallas_call(
        _ffn_kernel,
        grid=(m // tm, ff // tf),
        in_specs=[pl.BlockSpec((tm, d), lambda i, f: (i, 0)),
                  pl.BlockSpec((1, d), lambda i, f: (0, 0)),
                  pl.BlockSpec((d, tf), lambda i, f: (0, f)),
                  pl.BlockSpec((d, tf), lambda i, f: (0, f)),
                  pl.BlockSpec((tf, d), lambda i, f: (f, 0))],
        out_specs=pl.BlockSpec((tm, d), lambda i, f: (i, 0)),
        out_shape=jax.ShapeDtypeStruct((m, d), F32),
        scratch_shapes=[pltpu.VMEM((tm, d), BF16), pltpu.VMEM((tm, d), F32)],
        compiler_params=_params("parallel", "arbitrary"),
        name="swiglu_dense",
    )(x2, g, wg, wu, wd)


def _moe_kernel(x_ref, g_ref, dw_ref, wg_ref, wu_ref, wd_ref, o_ref, h_ref, acc_ref):
    e = pl.program_id(1)
    f = pl.program_id(2)

    @pl.when((e == 0) & (f == 0))
    def _():
        h_ref[...] = _rmsnorm_rows(x_ref[...], g_ref[...]).astype(BF16)
        acc_ref[...] = jnp.zeros(acc_ref.shape, F32)

    dw = dw_ref[...]
    lane = lax.broadcasted_iota(jnp.int32, dw.shape, 1)
    w_e = jnp.sum(jnp.where(lane == e, dw, 0.0), axis=-1, keepdims=True)
    acc_ref[...] += w_e * _swiglu_partial(h_ref[...], wg_ref[...], wu_ref[...], wd_ref[...])

    @pl.when((e == pl.num_programs(1) - 1) & (f == pl.num_programs(2) - 1))
    def _():
        o_ref[...] = x_ref[...] + acc_ref[...]


def _moe(x2, g, dense_w, wg, wu, wd, *, tm=512, tf=512):
    m, d = x2.shape
    ne, _, ff = wg.shape
    tm, tf = _tile(m, tm), _tile(ff, tf)
    return pl.pallas_call(
        _moe_kernel,
        grid=(m // tm, ne, ff // tf),
        in_specs=[pl.BlockSpec((tm, d), lambda i, e, f: (i, 0)),
                  pl.BlockSpec((1, d), lambda i, e, f: (0, 0)),
                  pl.BlockSpec((tm, LANES), lambda i, e, f: (i, 0)),
                  pl.BlockSpec((None, d, tf), lambda i, e, f: (e, 0, f)),
                  pl.BlockSpec((None, d, tf), lambda i, e, f: (e, 0, f)),
                  pl.BlockSpec((None, tf, d), lambda i, e, f: (e, f, 0))],
        out_specs=pl.BlockSpec((tm, d), lambda i, e, f: (i, 0)),
        out_shape=jax.ShapeDtypeStruct((m, d), F32),
        scratch_shapes=[pltpu.VMEM((tm, d), BF16), pltpu.VMEM((tm, d), F32)],
        compiler_params=_params("parallel", "arbitrary", "arbitrary"),
        name="swiglu_experts",
    )(x2, g, dense_w, wg, wu, wd)


def _pad_cols(a, n):
    return jnp.pad(a, ((0, 0), (0, n - a.shape[1])))


def _mixing_sublayer(x2, bsz, seq, norm_g, w_in, b_f, b_gate, conv_w, conv_b, cln_g, cln_b,
                     pool_w, pool_scale, q_g, k_g, w_br, w_o):
    d = x2.shape[1]
    row = lambda v: v.reshape(1, -1)
    w_main = w_in[:, :OFF_F].astype(BF16)
    w_f = _pad_cols(w_in[:, OFF_F:OFF_G], LANES)
    w_gate = w_in[:, OFF_G:].astype(BF16)
    bias_f = _pad_cols(row(b_f), LANES)

    z = _norm_proj(x2, row(norm_g), w_main, jnp.zeros((1, OFF_F), F32), out_dtype=BF16, tm=1024, tn=512)
    f_logit = _norm_proj(x2, row(norm_g), w_f, bias_f, precise=True, out_dtype=F32, tm=1024, tn=LANES)
    gates = _norm_proj(x2, row(norm_g), w_gate, row(b_gate), act="sigmoid", out_dtype=BF16, tm=1024, tn=512)

    z3 = z.reshape(bsz, seq, OFF_F)
    c3 = _log_forget_cumsum(f_logit.reshape(bsz, seq, LANES))
    y_ab = _mixers(z3, conv_w, row(conv_b), row(cln_g), row(cln_b), pool_w.astype(BF16), row(pool_scale))
    qp, kp = _qk_prep(z3, c3, row(q_g), row(k_g))
    y_c = _attention(qp, kp, z3)

    merged = _merge(y_ab.reshape(bsz * seq, -1), y_c.reshape(bsz * seq, -1), w_br.astype(BF16), gates)
    return _out_proj(merged, w_o.astype(BF16), x2)


def kernel(x, norm1_g, w_in, b_f, b_gate, conv_w, conv_b, cln_g, cln_b, pool_w, pool_scale, q_g, k_g, w_br, w_o,
           norm2_g, ffn_wg, ffn_wu, ffn_wd, router, exp_wg, exp_wu, exp_wd):
    bsz, seq, d = x.shape
    depth = w_in.shape[0]
    x2 = x.reshape(bsz * seq, d)
    for l in range(depth):
        x2 = _mixing_sublayer(x2, bsz, seq, norm1_g[l], w_in[l], b_f[l], b_gate[l], conv_w[l], conv_b[l],
                              cln_g[l], cln_b[l], pool_w[l], pool_scale[l], q_g[l], k_g[l], w_br[l], w_o[l])
        g2 = norm2_g[l].reshape(1, d)
        j = l // 2
        if l % 2 == 0:
            x2 = _ffn(x2, g2, ffn_wg[j].astype(BF16), ffn_wu[j].astype(BF16), ffn_wd[j].astype(BF16))
        else:
            dense_w = _norm_proj(x2, g2, _pad_cols(router[j], LANES), jnp.zeros((1, LANES), F32),
                                 act="top2", precise=True, out_dtype=F32, tm=1024, tn=LANES)
            x2 = _moe(x2, g2, dense_w, exp_wg[j].astype(BF16), exp_wu[j].astype(BF16), exp_wd[j].astype(BF16))
    return x2.reshape(bsz, seq, d)
```

```python
import functools

import jax
import jax.numpy as jnp
from jax import lax
from jax.experimental import pallas as pl
from jax.experimental.pallas import tpu as pltpu

EPS = 1e-6
CONV_WIDTH = 512
CONV_K = 31
POOL_WIDTH = 512
POOL_WINDOWS = (2, 4, 8, 16)
POOL_GROUP = POOL_WIDTH // len(POOL_WINDOWS)
ATT_HEADS = 8
HEAD_DIM = 128
ATT_WIDTH = ATT_HEADS * HEAD_DIM
N_BRANCH = 3
N_EXPERTS = 8
TOP_K = 2

OFF_B = 2 * CONV_WIDTH
OFF_Q = OFF_B + POOL_WIDTH
OFF_K = OFF_Q + ATT_WIDTH
OFF_V = OFF_K + ATT_WIDTH
OFF_F = OFF_V + ATT_WIDTH
OFF_G = OFF_F + ATT_HEADS

LANES = 128
SUBLANES = 8
HALO = 32
NEG_BIG = -1e30
VMEM_LIMIT_BYTES = 56 * 1024 * 1024
MOE_ROW_TILE = 512

F32 = jnp.float32
BF16 = jnp.bfloat16


def _tile(n, pref):
    t = min(n, pref)
    while n % t:
        t -= 1
    return t


def _params(*sem):
    return pltpu.CompilerParams(dimension_semantics=sem, vmem_limit_bytes=VMEM_LIMIT_BYTES)


def _rmsnorm_rows(x, g):
    ms = jnp.mean(x * x, axis=-1, keepdims=True)
    return x * lax.rsqrt(ms + EPS) * g


def _sigmoid(x):
    return 1.0 / (1.0 + jnp.exp(-x))


def _log_sigmoid(x):
    return jnp.minimum(x, 0.0) - jnp.log1p(jnp.exp(-jnp.abs(x)))


def _top2(logits):
    lane = lax.broadcasted_iota(jnp.int32, logits.shape, 1).astype(F32)
    lg = jnp.where(lane < N_EXPERTS, logits, NEG_BIG)
    m1 = jnp.max(lg, axis=-1, keepdims=True)
    i1 = jnp.min(jnp.where(lg == m1, lane, float(LANES)), axis=-1, keepdims=True)
    lg2 = jnp.where(lane == i1, NEG_BIG, lg)
    m2 = jnp.max(lg2, axis=-1, keepdims=True)
    i2 = jnp.min(jnp.where(lg2 == m2, lane, float(LANES)), axis=-1, keepdims=True)
    e = jnp.exp(m2 - m1)
    w1 = 1.0 / (1.0 + e)
    w2 = e / (1.0 + e)
    return jnp.where(lane == 0, i1, jnp.where(lane == 1, i2, jnp.where(lane == 2, w1, jnp.where(lane == 3, w2, 0.0))))


def _norm_proj_kernel(x_ref, g_ref, w_ref, b_ref, o_ref, h_ref, *, act, precise):
    @pl.when(pl.program_id(1) == 0)
    def _():
        h_ref[...] = _rmsnorm_rows(x_ref[...], g_ref[...]).astype(h_ref.dtype)

    y = jnp.dot(h_ref[...], w_ref[...], preferred_element_type=F32,
                precision=lax.Precision.HIGHEST if precise else None)
    y = y + b_ref[...]
    if act == "sigmoid":
        y = _sigmoid(y)
    elif act == "top2":
        y = _top2(y)
    o_ref[...] = y.astype(o_ref.dtype)


def _norm_proj(x2, g, w, b, *, act=None, precise=False, out_dtype, tm, tn):
    m, d = x2.shape
    n = w.shape[1]
    tm, tn = _tile(m, tm), _tile(n, tn)
    return pl.pallas_call(
        functools.partial(_norm_proj_kernel, act=act, precise=precise),
        grid=(m // tm, n // tn),
        in_specs=[pl.BlockSpec((tm, d), lambda i, j: (i, 0)),
                  pl.BlockSpec((1, d), lambda i, j: (0, 0)),
                  pl.BlockSpec((d, tn), lambda i, j: (0, j)),
                  pl.BlockSpec((1, tn), lambda i, j: (0, j))],
        out_specs=pl.BlockSpec((tm, tn), lambda i, j: (i, j)),
        out_shape=jax.ShapeDtypeStruct((m, n), out_dtype),
        scratch_shapes=[pltpu.VMEM((tm, d), F32 if precise else BF16)],
        compiler_params=_params("parallel", "arbitrary"),
        name="norm_proj_" + (act or "plain"),
    )(x2, g, w, b)


def _in_proj_kernel(x_ref, g_ref, w_ref, wf_ref, bf_ref, o_ref, f_ref, h_ref):
    @pl.when(pl.program_id(1) == 0)
    def _():
        h = _rmsnorm_rows(x_ref[...], g_ref[...]).astype(BF16)
        h_ref[...] = h
        f_ref[...] = jnp.dot(h, wf_ref[...], preferred_element_type=F32) + bf_ref[...]

    o_ref[...] = jnp.dot(h_ref[...], w_ref[...], preferred_element_type=F32).astype(o_ref.dtype)


def _in_proj(x2, g, w, w_f, b_f, *, tm=1024, tn=1536):
    m, d = x2.shape
    n = w.shape[1]
    tm, tn = _tile(m, tm), _tile(n, tn)
    return pl.pallas_call(
        _in_proj_kernel,
        grid=(m // tm, n // tn),
        in_specs=[pl.BlockSpec((tm, d), lambda i, j: (i, 0)),
                  pl.BlockSpec((1, d), lambda i, j: (0, 0)),
                  pl.BlockSpec((d, tn), lambda i, j: (0, j)),
                  pl.BlockSpec((d, LANES), lambda i, j: (0, 0)),
                  pl.BlockSpec((1, LANES), lambda i, j: (0, 0))],
        out_specs=[pl.BlockSpec((tm, tn), lambda i, j: (i, j)),
                   pl.BlockSpec((tm, LANES), lambda i, j: (i, 0))],
        out_shape=[jax.ShapeDtypeStruct((m, n), BF16), jax.ShapeDtypeStruct((m, LANES), F32)],
        scratch_shapes=[pltpu.VMEM((tm, d), BF16)],
        compiler_params=_params("parallel", "arbitrary"),
        name="in_proj",
    )(x2, g, w, w_f, b_f)


def _cumsum_kernel(f_ref, o_ref, *, chunk):
    s = f_ref.shape[0]
    row = lax.broadcasted_iota(jnp.int32, (chunk, chunk), 0)
    col = lax.broadcasted_iota(jnp.int32, (chunk, chunk), 1)
    tril = (row >= col).astype(F32)
    carry = jnp.zeros((1, f_ref.shape[1]), F32)
    for c in range(s // chunk):
        ls = _log_sigmoid(f_ref[c * chunk:(c + 1) * chunk, :])
        cs = jnp.dot(tril, ls, preferred_element_type=F32, precision=lax.Precision.HIGHEST) + carry
        o_ref[c * chunk:(c + 1) * chunk, :] = cs
        carry = cs[chunk - 1:chunk, :]


def _log_forget_cumsum(f3):
    b, s, n = f3.shape
    chunk = _tile(s, LANES)
    return pl.pallas_call(
        functools.partial(_cumsum_kernel, chunk=chunk),
        grid=(b,),
        in_specs=[pl.BlockSpec((None, s, n), lambda i: (i, 0, 0))],
        out_specs=pl.BlockSpec((None, s, n), lambda i: (i, 0, 0)),
        out_shape=jax.ShapeDtypeStruct((b, s, n), F32),
        compiler_params=_params("parallel"),
        name="log_forget_cumsum",
    )(f3)


def _shift_rows(win, p):
    if p == 0:
        return win
    return pltpu.roll(win, win.shape[0] - p, axis=0)


def _mixers_kernel(z_ref, cw_ref, cb_ref, lg_ref, lb_ref, pw_ref, ps_ref, o_ref, zp_ref, up_ref, cv_ref, *, rows):
    s = z_ref.shape[0]
    n_chunks = s // rows
    zp_ref[0:HALO, :] = jnp.zeros((HALO, CONV_WIDTH), F32)
    up_ref[0:HALO, :] = jnp.zeros((HALO, POOL_WIDTH), F32)

    def fill(c, _):
        r0 = pl.multiple_of(c * rows, rows)
        a = z_ref[pl.ds(r0, rows), 0:CONV_WIDTH].astype(F32)
        gt = z_ref[pl.ds(r0, rows), CONV_WIDTH:OFF_B].astype(F32)
        zp_ref[pl.ds(r0 + HALO, rows), :] = a * _sigmoid(gt)
        up_ref[pl.ds(r0 + HALO, rows), :] = z_ref[pl.ds(r0, rows), OFF_B:OFF_Q].astype(F32)
        return 0

    lax.fori_loop(0, n_chunks, fill, 0)

    lead = HALO - (CONV_K - 1)

    def mix(c, _):
        r0 = pl.multiple_of(c * rows, rows)
        for lg in range(CONV_WIDTH // LANES):
            ls = slice(lg * LANES, (lg + 1) * LANES)
            win = zp_ref[pl.ds(r0, rows + HALO), ls]
            acc = jnp.zeros((rows, LANES), F32) + cb_ref[:, ls]
            for p in range(SUBLANES):
                wp = _shift_rows(win, p)
                for k in range(CONV_K):
                    if (lead + k) % SUBLANES == p:
                        off = lead + k - p
                        acc = acc + cw_ref[k:k + 1, ls] * wp[off:off + rows, :]
            cv_ref[:, ls] = acc
        y = cv_ref[...]
        mu = jnp.mean(y, axis=-1, keepdims=True)
        yc = y - mu
        var = jnp.mean(yc * yc, axis=-1, keepdims=True)
        y = yc * lax.rsqrt(var + EPS) * lg_ref[...] + lb_ref[...]
        o_ref[pl.ds(r0, rows), 0:CONV_WIDTH] = (y * _sigmoid(y)).astype(o_ref.dtype)

        half = HALO // 2
        pos = (lax.broadcasted_iota(jnp.int32, (rows, LANES), 0) + (r0 + 1)).astype(F32)
        for gi, w in enumerate(POOL_WINDOWS):
            ls = slice(gi * POOL_GROUP, (gi + 1) * POOL_GROUP)
            win = up_ref[pl.ds(r0 + half, rows + half), ls]
            tot = win
            span = 1
            while span < w:
                tot = tot + pltpu.roll(tot, span, axis=0)
                span *= 2
            u = win[half:half + rows, :]
            p = tot[half:half + rows, :] / jnp.minimum(pos, float(w)) - u
            yb = jnp.dot(p.astype(BF16), pw_ref[gi], preferred_element_type=F32) * ps_ref[:, ls]
            o_ref[pl.ds(r0, rows), CONV_WIDTH + gi * POOL_GROUP:CONV_WIDTH + (gi + 1) * POOL_GROUP] = yb.astype(o_ref.dtype)
        return 0

    lax.fori_loop(0, n_chunks, mix, 0)


def _mixers(z3, conv_w, conv_b, cln_g, cln_b, pool_w, pool_scale):
    b, s, n = z3.shape
    rows = _tile(s, 256)
    width = OFF_Q
    assert n % width == 0
    return pl.pallas_call(
        functools.partial(_mixers_kernel, rows=rows),
        grid=(b,),
        in_specs=[pl.BlockSpec((None, s, width), lambda i: (i, 0, 0)),
                  pl.BlockSpec(conv_w.shape, lambda i: (0, 0)),
                  pl.BlockSpec(conv_b.shape, lambda i: (0, 0)),
                  pl.BlockSpec(cln_g.shape, lambda i: (0, 0)),
                  pl.BlockSpec(cln_b.shape, lambda i: (0, 0)),
                  pl.BlockSpec(pool_w.shape, lambda i: (0, 0, 0)),
                  pl.BlockSpec(pool_scale.shape, lambda i: (0, 0))],
        out_specs=pl.BlockSpec((None, s, CONV_WIDTH + POOL_WIDTH), lambda i: (i, 0, 0)),
        out_shape=jax.ShapeDtypeStruct((b, s, CONV_WIDTH + POOL_WIDTH), BF16),
        scratch_shapes=[pltpu.VMEM((s + HALO, CONV_WIDTH), F32),
                        pltpu.VMEM((s + HALO, POOL_WIDTH), F32),
                        pltpu.VMEM((rows, CONV_WIDTH), F32)],
        compiler_params=_params("parallel"),
        name="conv_pool_mixers",
    )(z3, conv_w, conv_b, cln_g, cln_b, pool_w, pool_scale)


def _split3(c):
    hi = c.astype(BF16).astype(F32)
    r1 = c - hi
    mid = r1.astype(BF16).astype(F32)
    return hi, mid, r1 - mid


def _qk_prep_kernel(qa_ref, qb_ref, ka_ref, kb_ref, c_ref, qg_ref, kg_ref, qo_ref, ko_ref):
    ts = c_ref.shape[0]
    scale = HEAD_DIM ** -0.5
    hi, mid, lo = _split3(c_ref[...])
    lane = lax.broadcasted_iota(jnp.int32, (ts, LANES), 1)
    half = ATT_HEADS // 2
    for h in range(ATT_HEADS):
        q_src, k_src = (qa_ref, ka_ref) if h < half else (qb_ref, kb_ref)
        ls = slice((h % half) * HEAD_DIM, (h % half + 1) * HEAD_DIM)
        qn = _rmsnorm_rows(q_src[:, ls].astype(F32), qg_ref[...]) * scale
        kn = _rmsnorm_rows(k_src[:, ls].astype(F32), kg_ref[...])
        ch, cm, cl = hi[:, h:h + 1], mid[:, h:h + 1], lo[:, h:h + 1]
        ones = jnp.where(lane < 6, 1.0, 0.0)
        q_ext = jnp.where(lane == 0, ch, jnp.where(lane == 1, cm, jnp.where(lane == 2, cl, ones)))
        k_ext = jnp.where(lane == 3, -ch, jnp.where(lane == 4, -cm, jnp.where(lane == 5, -cl, ones)))
        qo_ref[h, :, 0:HEAD_DIM] = qn.astype(BF16)
        qo_ref[h, :, HEAD_DIM:2 * HEAD_DIM] = q_ext.astype(BF16)
        ko_ref[h, :, 0:HEAD_DIM] = kn.astype(BF16)
        ko_ref[h, :, HEAD_DIM:2 * HEAD_DIM] = k_ext.astype(BF16)


def _qk_prep(z3, c3, q_g, k_g):
    b, s, _ = z3.shape
    ts = _tile(s, 512)
    blk = (ATT_HEADS // 2) * HEAD_DIM
    qb0, kb0 = OFF_Q // blk, OFF_K // blk
    assert OFF_Q % blk == 0 and OFF_K % blk == 0
    out = jax.ShapeDtypeStruct((b, ATT_HEADS, s, 2 * HEAD_DIM), BF16)
    zspec = lambda cb: pl.BlockSpec((None, ts, blk), lambda i, j, cb=cb: (i, j, cb))
    ospec = pl.BlockSpec((None, ATT_HEADS, ts, 2 * HEAD_DIM), lambda i, j: (i, 0, j, 0))
    return pl.pallas_call(
        _qk_prep_kernel,
        grid=(b, s // ts),
        in_specs=[zspec(qb0), zspec(qb0 + 1), zspec(kb0), zspec(kb0 + 1),
                  pl.BlockSpec((None, ts, LANES), lambda i, j: (i, j, 0)),
                  pl.BlockSpec((1, HEAD_DIM), lambda i, j: (0, 0)),
                  pl.BlockSpec((1, HEAD_DIM), lambda i, j: (0, 0))],
        out_specs=[ospec, ospec],
        out_shape=[out, out],
        compiler_params=_params("parallel", "parallel"),
        name="qk_prep",
    )(z3, z3, z3, z3, c3, q_g, k_g)


def _attn_kernel(q_ref, k_ref, v_ref, o_ref, *, tq):
    s_len = q_ref.shape[0]
    for i in range(s_len // tq):
        n = (i + 1) * tq
        q = q_ref[i * tq:(i + 1) * tq, :]
        s = lax.dot_general(q, k_ref[0:n, :], (((1,), (1,)), ((), ())), preferred_element_type=F32)
        row = lax.broadcasted_iota(jnp.int32, s.shape, 0) + i * tq
        col = lax.broadcasted_iota(jnp.int32, s.shape, 1)
        s = jnp.where(col <= row, s, NEG_BIG)
        m = jnp.max(s, axis=-1, keepdims=True)
        p = jnp.exp(s - m)
        l = jnp.sum(p, axis=-1, keepdims=True)
        acc = jnp.dot(p.astype(BF16), v_ref[0:n, :], preferred_element_type=F32)
        o_ref[i * tq:(i + 1) * tq, :] = (acc / l).astype(o_ref.dtype)


def _attention(qp, kp, z3):
    b, h, s, dk = qp.shape
    tq = _tile(s, 512)
    v0 = OFF_V // HEAD_DIM
    return pl.pallas_call(
        functools.partial(_attn_kernel, tq=tq),
        grid=(b, h),
        in_specs=[pl.BlockSpec((None, None, s, dk), lambda bi, hi: (bi, hi, 0, 0)),
                  pl.BlockSpec((None, None, s, dk), lambda bi, hi: (bi, hi, 0, 0)),
                  pl.BlockSpec((None, s, HEAD_DIM), lambda bi, hi: (bi, 0, v0 + hi))],
        out_specs=pl.BlockSpec((None, s, HEAD_DIM), lambda bi, hi: (bi, 0, hi)),
        out_shape=jax.ShapeDtypeStruct((b, s, ATT_WIDTH), BF16),
        compiler_params=_params("parallel", "parallel"),
        name="fox_attention",
    )(qp, kp, z3)


def _merge_kernel(ya_ref, yb_ref, yc_ref, wa_ref, wb_ref, wc_ref, g0_ref, g1_ref, g2_ref, o_ref):
    a = jnp.dot(ya_ref[...], wa_ref[...], preferred_element_type=F32)
    b = jnp.dot(yb_ref[...], wb_ref[...], preferred_element_type=F32)
    c = jnp.dot(yc_ref[...], wc_ref[...], preferred_element_type=F32)
    merged = g0_ref[...].astype(F32) * a + g1_ref[...].astype(F32) * b + g2_ref[...].astype(F32) * c
    o_ref[...] = merged.astype(o_ref.dtype)


def _merge(y_ab, y_c, w_br, gates, *, tm=1024, tn=1024):
    m = y_ab.shape[0]
    d = w_br.shape[1]
    tm, tn = _tile(m, tm), _tile(d, tn)
    nj = d // tn
    return pl.pallas_call(
        _merge_kernel,
        grid=(m // tm, nj),
        in_specs=[pl.BlockSpec((tm, CONV_WIDTH), lambda i, j: (i, 0)),
                  pl.BlockSpec((tm, POOL_WIDTH), lambda i, j: (i, 1)),
                  pl.BlockSpec((tm, ATT_WIDTH), lambda i, j: (i, 0)),
                  pl.BlockSpec((CONV_WIDTH, tn), lambda i, j: (0, j)),
                  pl.BlockSpec((POOL_WIDTH, tn), lambda i, j: (1, j)),
                  pl.BlockSpec((ATT_WIDTH, tn), lambda i, j: (1, j)),
                  pl.BlockSpec((tm, tn), lambda i, j: (i, j)),
                  pl.BlockSpec((tm, tn), lambda i, j: (i, nj + j)),
                  pl.BlockSpec((tm, tn), lambda i, j: (i, 2 * nj + j))],
        out_specs=pl.BlockSpec((tm, tn), lambda i, j: (i, j)),
        out_shape=jax.ShapeDtypeStruct((m, d), BF16),
        compiler_params=_params("parallel", "arbitrary"),
        name="branch_merge",
    )(y_ab, y_ab, y_c, w_br, w_br, w_br, gates, gates, gates)


def _out_proj_kernel(a_ref, w_ref, x_ref, o_ref):
    o_ref[...] = x_ref[...] + jnp.dot(a_ref[...], w_ref[...], preferred_element_type=F32)


def _out_proj(a, w, x2, *, tm=1024, tn=1024):
    m, k = a.shape
    n = w.shape[1]
    tm, tn = _tile(m, tm), _tile(n, tn)
    return pl.pallas_call(
        _out_proj_kernel,
        grid=(m // tm, n // tn),
        in_specs=[pl.BlockSpec((tm, k), lambda i, j: (i, 0)),
                  pl.BlockSpec((k, tn), lambda i, j: (0, j)),
                  pl.BlockSpec((tm, tn), lambda i, j: (i, j))],
        out_specs=pl.BlockSpec((tm, tn), lambda i, j: (i, j)),
        out_shape=jax.ShapeDtypeStruct((m, n), F32),
        compiler_params=_params("parallel", "arbitrary"),
        name="out_proj_residual",
    )(a, w, x2)


def _swiglu_partial(h, wg, wu, wd):
    a = jnp.dot(h, wg, preferred_element_type=F32)
    u = jnp.dot(h, wu, preferred_element_type=F32)
    act = (a * _sigmoid(a) * u).astype(BF16)
    return jnp.dot(act, wd, preferred_element_type=F32)


def _ffn_kernel(x_ref, g_ref, wg_ref, wu_ref, wd_ref, o_ref, h_ref, acc_ref):
    f = pl.program_id(1)

    @pl.when(f == 0)
    def _():
        h_ref[...] = _rmsnorm_rows(x_ref[...], g_ref[...]).astype(BF16)
        acc_ref[...] = jnp.zeros(acc_ref.shape, F32)

    acc_ref[...] += _swiglu_partial(h_ref[...], wg_ref[...], wu_ref[...], wd_ref[...])

    @pl.when(f == pl.num_programs(1) - 1)
    def _():
        o_ref[...] = x_ref[...] + acc_ref[...]


def _ffn(x2, g, wg, wu, wd, *, tm=512, tf=512):
    m, d = x2.shape
    ff = wg.shape[1]
    tm, tf = _tile(m, tm), _tile(ff, tf)
    return pl.pallas_call(
        _ffn_kernel,
        grid=(m // tm, ff // tf),
        in_specs=[pl.BlockSpec((tm, d), lambda i, f: (i, 0)),
                  pl.BlockSpec((1, d), lambda i, f: (0, 0)),
                  pl.BlockSpec((d, tf), lambda i, f: (0, f)),
                  pl.BlockSpec((d, tf), lambda i, f: (0, f)),
                  pl.BlockSpec((tf, d), lambda i, f: (f, 0))],
        out_specs=pl.BlockSpec((tm, d), lambda i, f: (i, 0)),
        out_shape=jax.ShapeDtypeStruct((m, d), F32),
        scratch_shapes=[pltpu.VMEM((tm, d), BF16), pltpu.VMEM((tm, d), F32)],
        compiler_params=_params("parallel", "arbitrary"),
        name="swiglu_dense",
    )(x2, g, wg, wu, wd)


def _gather_rows_kernel(idx_ref, src_ref, dst_ref, sem, *, chunk):
    s = pl.program_id(0)
    base = s * chunk

    def row_copy(src_row, dst_row):
        return pltpu.make_async_copy(src_ref.at[pl.ds(src_row, 1)], dst_ref.at[pl.ds(dst_row, 1)], sem)

    def issue(r, c):
        row_copy(idx_ref[base + r], base + r).start()
        return c

    def drain(r, c):
        row_copy(0, 0).wait()
        return c

    lax.fori_loop(0, chunk, issue, 0, unroll=8)

    @pl.when(s > 0)
    def _():
        lax.fori_loop(0, chunk, drain, 0, unroll=8)

    @pl.when(s == pl.num_programs(0) - 1)
    def _():
        lax.fori_loop(0, chunk, drain, 0, unroll=8)


def _gather_rows(idx, src, *, chunk=512):
    n = idx.shape[0]
    d = src.shape[1]
    chunk = _tile(n, chunk)
    return pl.pallas_call(
        functools.partial(_gather_rows_kernel, chunk=chunk),
        grid_spec=pltpu.PrefetchScalarGridSpec(
            num_scalar_prefetch=1,
            grid=(n // chunk,),
            in_specs=[pl.BlockSpec(memory_space=pl.ANY)],
            out_specs=pl.BlockSpec(memory_space=pl.ANY),
            scratch_shapes=[pltpu.SemaphoreType.DMA(())]),
        out_shape=jax.ShapeDtypeStruct((n, d), src.dtype),
        compiler_params=_params("arbitrary"),
        name="gather_expert_rows",
    )(idx, src)


def _moe_ffn_kernel(te_ref, nu_ref, x_ref, g_ref, rw_ref, wg_ref, wu_ref, wd_ref, o_ref, h_ref, acc_ref):
    i = pl.program_id(0)
    f = pl.program_id(1)
    last = pl.num_programs(1) - 1
    used = i < nu_ref[0]

    @pl.when(used & (f == 0))
    def _():
        h_ref[...] = _rmsnorm_rows(x_ref[...], g_ref[...]).astype(BF16)
        acc_ref[...] = jnp.zeros(acc_ref.shape, F32)

    @pl.when(used)
    def _():
        acc_ref[...] += _swiglu_partial(h_ref[...], wg_ref[...], wu_ref[...], wd_ref[...])

    @pl.when(used & (f == last))
    def _():
        o_ref[...] = rw_ref[...] * acc_ref[...]

    @pl.when(jnp.logical_not(used) & (f == last))
    def _():
        o_ref[...] = jnp.zeros(o_ref.shape, F32)


def _moe_ffn(tile_expert, n_used, xs, g, row_w, wg, wu, wd, *, tm, tf=512):
    mp, d = xs.shape
    ff = wg.shape[2]
    tf = _tile(ff, tf)
    nf = ff // tf

    def wspec(shape, fdim):
        def imap(i, f, te, nu):
            fi = jnp.where(i < nu[0], f, nf - 1)
            return (te[i], fi, 0) if fdim == 1 else (te[i], 0, fi)
        return pl.BlockSpec(shape, imap)

    return pl.pallas_call(
        _moe_ffn_kernel,
        grid_spec=pltpu.PrefetchScalarGridSpec(
            num_scalar_prefetch=2,
            grid=(mp // tm, nf),
            in_specs=[pl.BlockSpec((tm, d), lambda i, f, te, nu: (i, 0)),
                      pl.BlockSpec((1, d), lambda i, f, te, nu: (0, 0)),
                      pl.BlockSpec((tm, 1), lambda i, f, te, nu: (i, 0)),
                      wspec((None, d, tf), 2),
                      wspec((None, d, tf), 2),
                      wspec((None, tf, d), 1)],
            out_specs=pl.BlockSpec((tm, d), lambda i, f, te, nu: (i, 0)),
            scratch_shapes=[pltpu.VMEM((tm, d), BF16), pltpu.VMEM((tm, d), F32)]),
        out_shape=jax.ShapeDtypeStruct((mp, d), F32),
        compiler_params=_params("parallel", "arbitrary"),
        name="swiglu_experts_grouped",
    )(tile_expert, n_used, xs, g, row_w, wg, wu, wd)


def _combine_kernel(p1_ref, p2_ref, x_ref, y_ref, o_ref, buf_ref, sem, *, tc):
    base = pl.program_id(0) * tc

    def row_copy(src_row, slot, r):
        return pltpu.make_async_copy(y_ref.at[pl.ds(src_row, 1)], buf_ref.at[slot, pl.ds(r, 1)], sem)

    def issue(r, c):
        row_copy(p1_ref[base + r], 0, r).start()
        row_copy(p2_ref[base + r], 1, r).start()
        return c

    def drain(r, c):
        row_copy(0, 0, r).wait()
        row_copy(0, 1, r).wait()
        return c

    lax.fori_loop(0, tc, issue, 0, unroll=8)
    lax.fori_loop(0, tc, drain, 0, unroll=8)
    o_ref[...] = x_ref[...] + buf_ref[0] + buf_ref[1]


def _combine(p1, p2, x2, y, *, tc=512):
    m, d = x2.shape
    tc = _tile(m, tc)
    return pl.pallas_call(
        functools.partial(_combine_kernel, tc=tc),
        grid_spec=pltpu.PrefetchScalarGridSpec(
            num_scalar_prefetch=2,
            grid=(m // tc,),
            in_specs=[pl.BlockSpec((tc, d), lambda i, p1, p2: (i, 0)),
                      pl.BlockSpec(memory_space=pl.ANY)],
            out_specs=pl.BlockSpec((tc, d), lambda i, p1, p2: (i, 0)),
            scratch_shapes=[pltpu.VMEM((TOP_K, tc, d), F32), pltpu.SemaphoreType.DMA(())]),
        out_shape=jax.ShapeDtypeStruct((m, d), F32),
        compiler_params=_params("arbitrary"),
        name="combine_expert_rows",
    )(p1, p2, x2, y)


def _route(sel, tm):
    m = sel.shape[0]
    e_flat = sel[:, :TOP_K].astype(jnp.int32).T.reshape(-1)
    w_flat = sel[:, TOP_K:2 * TOP_K].T.reshape(-1)
    onehot = (e_flat[:, None] == jnp.arange(N_EXPERTS, dtype=jnp.int32)[None, :]).astype(jnp.int32)
    csum = jnp.cumsum(onehot, axis=0)
    rank = jnp.sum((csum - onehot) * onehot, axis=1)
    counts = csum[-1]
    padded = ((counts + tm - 1) // tm) * tm
    ends = jnp.cumsum(padded)
    dest = (ends - padded)[e_flat] + rank
    mp = TOP_K * m + N_EXPERTS * tm
    n_tiles = mp // tm
    tok = jnp.tile(jnp.arange(m, dtype=jnp.int32), TOP_K)
    src_tok = jnp.zeros((mp,), jnp.int32).at[dest].set(tok)
    row_w = jnp.zeros((mp,), F32).at[dest].set(w_flat)
    n_used = (ends[-1] // tm).astype(jnp.int32)
    tile_id = jnp.arange(n_tiles, dtype=jnp.int32)
    tile_e = jnp.minimum(jnp.searchsorted(ends, tile_id * tm, side="right"), N_EXPERTS - 1).astype(jnp.int32)
    tile_e = jnp.where(tile_id < n_used, tile_e, tile_e[jnp.maximum(n_used - 1, 0)])
    return src_tok, row_w.reshape(mp, 1), tile_e, n_used.reshape(1), dest[:m], dest[m:]


def _moe(x2, g, router_w, wg, wu, wd):
    m, d = x2.shape
    tm = _tile(m, MOE_ROW_TILE)
    sel = _norm_proj(x2, g, _pad_cols(router_w, LANES), jnp.zeros((1, LANES), F32),
                     act="top2", precise=True, out_dtype=F32, tm=1024, tn=LANES)
    src_tok, row_w, tile_e, n_used, p1, p2 = _route(sel, tm)
    xs = _gather_rows(src_tok, x2)
    y = _moe_ffn(tile_e, n_used, xs, g, row_w, wg, wu, wd, tm=tm)
    return _combine(p1, p2, x2, y)


def _pad_cols(a, n):
    return jnp.pad(a, ((0, 0), (0, n - a.shape[1])))


def _mixing_sublayer(x2, bsz, seq, norm_g, w_in, b_f, b_gate, conv_w, conv_b, cln_g, cln_b,
                     pool_w, pool_scale, q_g, k_g, w_br, w_o):
    row = lambda v: v.reshape(1, -1)
    w_main = w_in[:, :OFF_F].astype(BF16)
    w_f = _pad_cols(w_in[:, OFF_F:OFF_G], LANES).astype(BF16)
    w_gate = w_in[:, OFF_G:].astype(BF16)

    z, f_logit = _in_proj(x2, row(norm_g), w_main, w_f, _pad_cols(row(b_f), LANES))
    gates = _norm_proj(x2, row(norm_g), w_gate, row(b_gate), act="sigmoid", out_dtype=BF16, tm=1024, tn=2048)

    z3 = z.reshape(bsz, seq, OFF_F)
    c3 = _log_forget_cumsum(f_logit.reshape(bsz, seq, LANES))
    y_ab = _mixers(z3, conv_w, row(conv_b), row(cln_g), row(cln_b), pool_w.astype(BF16), row(pool_scale))
    qp, kp = _qk_prep(z3, c3, row(q_g), row(k_g))
    y_c = _attention(qp, kp, z3)

    merged = _merge(y_ab.reshape(bsz * seq, -1), y_c.reshape(bsz * seq, -1), w_br.astype(BF16), gates)
    return _out_proj(merged, w_o.astype(BF16), x2)


def kernel(x, norm1_g, w_in, b_f, b_gate, conv_w, conv_b, cln_g, cln_b, pool_w, pool_scale, q_g, k_g, w_br, w_o,
           norm2_g, ffn_wg, ffn_wu, ffn_wd, router, exp_wg, exp_wu, exp_wd):
    bsz, seq, d = x.shape
    depth = w_in.shape[0]
    x2 = x.reshape(bsz * seq, d)
    for l in range(depth):
        x2 = _mixing_sublayer(x2, bsz, seq, norm1_g[l], w_in[l], b_f[l], b_gate[l], conv_w[l], conv_b[l],
                              cln_g[l], cln_b[l], pool_w[l], pool_scale[l], q_g[l], k_g[l], w_br[l], w_o[l])
        g2 = norm2_g[l].reshape(1, d)
        j = l // 2
        if l % 2 == 0:
            x2 = _ffn(x2, g2, ffn_wg[j].astype(BF16), ffn_wu[j].astype(BF16), ffn_wd[j].astype(BF16))
        else:
            x2 = _moe(x2, g2, router[j], exp_wg[j].astype(BF16), exp_wu[j].astype(BF16), exp_wd[j].astype(BF16))
    return x2.reshape(bsz, seq, d)
```

```python
import functools

import jax
import jax.numpy as jnp
from jax import lax
from jax.experimental import pallas as pl
from jax.experimental.pallas import tpu as pltpu

EPS = 1e-6
CONV_WIDTH = 512
CONV_K = 31
POOL_WIDTH = 512
POOL_WINDOWS = (2, 4, 8, 16)
POOL_GROUP = POOL_WIDTH // len(POOL_WINDOWS)
ATT_HEADS = 8
HEAD_DIM = 128
ATT_WIDTH = ATT_HEADS * HEAD_DIM
N_BRANCH = 3
N_EXPERTS = 8
TOP_K = 2

OFF_B = 2 * CONV_WIDTH
OFF_Q = OFF_B + POOL_WIDTH
OFF_K = OFF_Q + ATT_WIDTH
OFF_V = OFF_K + ATT_WIDTH
OFF_F = OFF_V + ATT_WIDTH
OFF_G = OFF_F + ATT_HEADS

LANES = 128
SUBLANES = 8
HALO = 32
NEG_BIG = -1e30
VMEM_LIMIT_BYTES = 56 * 1024 * 1024
MOE_ROW_TILE = 512

F32 = jnp.float32
BF16 = jnp.bfloat16


def _tile(n, pref):
    t = min(n, pref)
    while n % t:
        t -= 1
    return t


def _params(*sem):
    return pltpu.CompilerParams(dimension_semantics=sem, vmem_limit_bytes=VMEM_LIMIT_BYTES)


def _rmsnorm_rows(x, g):
    ms = jnp.mean(x * x, axis=-1, keepdims=True)
    return x * lax.rsqrt(ms + EPS) * g


def _sigmoid(x):
    return 1.0 / (1.0 + jnp.exp(-x))


def _log_sigmoid(x):
    return jnp.minimum(x, 0.0) - jnp.log1p(jnp.exp(-jnp.abs(x)))


def _top2(logits):
    lane = lax.broadcasted_iota(jnp.int32, logits.shape, 1).astype(F32)
    lg = jnp.where(lane < N_EXPERTS, logits, NEG_BIG)
    m1 = jnp.max(lg, axis=-1, keepdims=True)
    i1 = jnp.min(jnp.where(lg == m1, lane, float(LANES)), axis=-1, keepdims=True)
    lg2 = jnp.where(lane == i1, NEG_BIG, lg)
    m2 = jnp.max(lg2, axis=-1, keepdims=True)
    i2 = jnp.min(jnp.where(lg2 == m2, lane, float(LANES)), axis=-1, keepdims=True)
    e = jnp.exp(m2 - m1)
    w1 = 1.0 / (1.0 + e)
    w2 = e / (1.0 + e)
    return jnp.where(lane == 0, i1, jnp.where(lane == 1, i2, jnp.where(lane == 2, w1, jnp.where(lane == 3, w2, 0.0))))


def _norm_proj_kernel(x_ref, g_ref, w_ref, b_ref, o_ref, h_ref, *, act, precise):
    @pl.when(pl.program_id(1) == 0)
    def _():
        h_ref[...] = _rmsnorm_rows(x_ref[...], g_ref[...]).astype(h_ref.dtype)

    y = jnp.dot(h_ref[...], w_ref[...], preferred_element_type=F32,
                precision=lax.Precision.HIGHEST if precise else None)
    y = y + b_ref[...]
    if act == "sigmoid":
        y = _sigmoid(y)
    elif act == "top2":
        y = _top2(y)
    o_ref[...] = y.astype(o_ref.dtype)


def _norm_proj(x2, g, w, b, *, act=None, precise=False, out_dtype, tm, tn):
    m, d = x2.shape
    n = w.shape[1]
    tm, tn = _tile(m, tm), _tile(n, tn)
    return pl.pallas_call(
        functools.partial(_norm_proj_kernel, act=act, precise=precise),
        grid=(m // tm, n // tn),
        in_specs=[pl.BlockSpec((tm, d), lambda i, j: (i, 0)),
                  pl.BlockSpec((1, d), lambda i, j: (0, 0)),
                  pl.BlockSpec((d, tn), lambda i, j: (0, j)),
                  pl.BlockSpec((1, tn), lambda i, j: (0, j))],
        out_specs=pl.BlockSpec((tm, tn), lambda i, j: (i, j)),
        out_shape=jax.ShapeDtypeStruct((m, n), out_dtype),
        scratch_shapes=[pltpu.VMEM((tm, d), F32 if precise else BF16)],
        compiler_params=_params("parallel", "arbitrary"),
        name="norm_proj_" + (act or "plain"),
    )(x2, g, w, b)


def _in_proj_kernel(x_ref, g_ref, w_ref, wf_ref, bf_ref, o_ref, f_ref, h_ref):
    @pl.when(pl.program_id(1) == 0)
    def _():
        h = _rmsnorm_rows(x_ref[...], g_ref[...]).astype(BF16)
        h_ref[...] = h
        f_ref[...] = jnp.dot(h, wf_ref[...], preferred_element_type=F32) + bf_ref[...]

    o_ref[...] = jnp.dot(h_ref[...], w_ref[...], preferred_element_type=F32).astype(o_ref.dtype)


def _in_proj(x2, g, w, w_f, b_f, *, tm=1024, tn=1536):
    m, d = x2.shape
    n = w.shape[1]
    tm, tn = _tile(m, tm), _tile(n, tn)
    return pl.pallas_call(
        _in_proj_kernel,
        grid=(m // tm, n // tn),
        in_specs=[pl.BlockSpec((tm, d), lambda i, j: (i, 0)),
                  pl.BlockSpec((1, d), lambda i, j: (0, 0)),
                  pl.BlockSpec((d, tn), lambda i, j: (0, j)),
                  pl.BlockSpec((d, LANES), lambda i, j: (0, 0)),
                  pl.BlockSpec((1, LANES), lambda i, j: (0, 0))],
        out_specs=[pl.BlockSpec((tm, tn), lambda i, j: (i, j)),
                   pl.BlockSpec((tm, LANES), lambda i, j: (i, 0))],
        out_shape=[jax.ShapeDtypeStruct((m, n), BF16), jax.ShapeDtypeStruct((m, LANES), F32)],
        scratch_shapes=[pltpu.VMEM((tm, d), BF16)],
        compiler_params=_params("parallel", "arbitrary"),
        name="in_proj",
    )(x2, g, w, w_f, b_f)


def _cumsum_kernel(f_ref, o_ref, *, chunk):
    s = f_ref.shape[0]
    row = lax.broadcasted_iota(jnp.int32, (chunk, chunk), 0)
    col = lax.broadcasted_iota(jnp.int32, (chunk, chunk), 1)
    tril = (row >= col).astype(F32)
    carry = jnp.zeros((1, f_ref.shape[1]), F32)
    for c in range(s // chunk):
        ls = _log_sigmoid(f_ref[c * chunk:(c + 1) * chunk, :])
        cs = jnp.dot(tril, ls, preferred_element_type=F32, precision=lax.Precision.HIGHEST) + carry
        o_ref[c * chunk:(c + 1) * chunk, :] = cs
        carry = cs[chunk - 1:chunk, :]


def _log_forget_cumsum(f3):
    b, s, n = f3.shape
    chunk = _tile(s, LANES)
    return pl.pallas_call(
        functools.partial(_cumsum_kernel, chunk=chunk),
        grid=(b,),
        in_specs=[pl.BlockSpec((None, s, n), lambda i: (i, 0, 0))],
        out_specs=pl.BlockSpec((None, s, n), lambda i: (i, 0, 0)),
        out_shape=jax.ShapeDtypeStruct((b, s, n), F32),
        compiler_params=_params("parallel"),
        name="log_forget_cumsum",
    )(f3)


def _shift_rows(win, p):
    if p == 0:
        return win
    return pltpu.roll(win, win.shape[0] - p, axis=0)


def _mixers_kernel(z_ref, cw_ref, cb_ref, lg_ref, lb_ref, pw_ref, ps_ref, o_ref, zp_ref, up_ref, cv_ref, *, rows):
    s = z_ref.shape[0]
    n_chunks = s // rows
    zp_ref[0:HALO, :] = jnp.zeros((HALO, CONV_WIDTH), F32)
    up_ref[0:HALO, :] = jnp.zeros((HALO, POOL_WIDTH), F32)

    def fill(c, _):
        r0 = pl.multiple_of(c * rows, rows)
        a = z_ref[pl.ds(r0, rows), 0:CONV_WIDTH].astype(F32)
        gt = z_ref[pl.ds(r0, rows), CONV_WIDTH:OFF_B].astype(F32)
        zp_ref[pl.ds(r0 + HALO, rows), :] = a * _sigmoid(gt)
        up_ref[pl.ds(r0 + HALO, rows), :] = z_ref[pl.ds(r0, rows), OFF_B:OFF_Q].astype(F32)
        return 0

    lax.fori_loop(0, n_chunks, fill, 0)

    lead = HALO - (CONV_K - 1)

    def mix(c, _):
        r0 = pl.multiple_of(c * rows, rows)
        for lg in range(CONV_WIDTH // LANES):
            ls = slice(lg * LANES, (lg + 1) * LANES)
            win = zp_ref[pl.ds(r0, rows + HALO), ls]
            acc = jnp.zeros((rows, LANES), F32) + cb_ref[:, ls]
            for p in range(SUBLANES):
                wp = _shift_rows(win, p)
                for k in range(CONV_K):
                    if (lead + k) % SUBLANES == p:
                        off = lead + k - p
                        acc = acc + cw_ref[k:k + 1, ls] * wp[off:off + rows, :]
            cv_ref[:, ls] = acc
        y = cv_ref[...]
        mu = jnp.mean(y, axis=-1, keepdims=True)
        yc = y - mu
        var = jnp.mean(yc * yc, axis=-1, keepdims=True)
        y = yc * lax.rsqrt(var + EPS) * lg_ref[...] + lb_ref[...]
        o_ref[pl.ds(r0, rows), 0:CONV_WIDTH] = (y * _sigmoid(y)).astype(o_ref.dtype)

        half = HALO // 2
        pos = (lax.broadcasted_iota(jnp.int32, (rows, LANES), 0) + (r0 + 1)).astype(F32)
        for gi, w in enumerate(POOL_WINDOWS):
            ls = slice(gi * POOL_GROUP, (gi + 1) * POOL_GROUP)
            win = up_ref[pl.ds(r0 + half, rows + half), ls]
            tot = win
            span = 1
            while span < w:
                tot = tot + pltpu.roll(tot, span, axis=0)
                span *= 2
            u = win[half:half + rows, :]
            p = tot[half:half + rows, :] / jnp.minimum(pos, float(w)) - u
            yb = jnp.dot(p.astype(BF16), pw_ref[gi], preferred_element_type=F32) * ps_ref[:, ls]
            o_ref[pl.ds(r0, rows), CONV_WIDTH + gi * POOL_GROUP:CONV_WIDTH + (gi + 1) * POOL_GROUP] = yb.astype(o_ref.dtype)
        return 0

    lax.fori_loop(0, n_chunks, mix, 0)


def _mixers(z3, conv_w, conv_b, cln_g, cln_b, pool_w, pool_scale):
    b, s, n = z3.shape
    rows = _tile(s, 256)
    width = OFF_Q
    assert n % width == 0
    return pl.pallas_call(
        functools.partial(_mixers_kernel, rows=rows),
        grid=(b,),
        in_specs=[pl.BlockSpec((None, s, width), lambda i: (i, 0, 0)),
                  pl.BlockSpec(conv_w.shape, lambda i: (0, 0)),
                  pl.BlockSpec(conv_b.shape, lambda i: (0, 0)),
                  pl.BlockSpec(cln_g.shape, lambda i: (0, 0)),
                  pl.BlockSpec(cln_b.shape, lambda i: (0, 0)),
                  pl.BlockSpec(pool_w.shape, lambda i: (0, 0, 0)),
                  pl.BlockSpec(pool_scale.shape, lambda i: (0, 0))],
        out_specs=pl.BlockSpec((None, s, CONV_WIDTH + POOL_WIDTH), lambda i: (i, 0, 0)),
        out_shape=jax.ShapeDtypeStruct((b, s, CONV_WIDTH + POOL_WIDTH), BF16),
        scratch_shapes=[pltpu.VMEM((s + HALO, CONV_WIDTH), F32),
                        pltpu.VMEM((s + HALO, POOL_WIDTH), F32),
                        pltpu.VMEM((rows, CONV_WIDTH), F32)],
        compiler_params=_params("parallel"),
        name="conv_pool_mixers",
    )(z3, conv_w, conv_b, cln_g, cln_b, pool_w, pool_scale)


def _split3(c):
    hi = c.astype(BF16).astype(F32)
    r1 = c - hi
    mid = r1.astype(BF16).astype(F32)
    return hi, mid, r1 - mid


def _qk_prep_kernel(qa_ref, qb_ref, ka_ref, kb_ref, c_ref, qg_ref, kg_ref, qo_ref, ko_ref):
    ts = c_ref.shape[0]
    scale = HEAD_DIM ** -0.5
    hi, mid, lo = _split3(c_ref[...])
    lane = lax.broadcasted_iota(jnp.int32, (ts, LANES), 1)
    half = ATT_HEADS // 2
    for h in range(ATT_HEADS):
        q_src, k_src = (qa_ref, ka_ref) if h < half else (qb_ref, kb_ref)
        ls = slice((h % half) * HEAD_DIM, (h % half + 1) * HEAD_DIM)
        qn = _rmsnorm_rows(q_src[:, ls].astype(F32), qg_ref[...]) * scale
        kn = _rmsnorm_rows(k_src[:, ls].astype(F32), kg_ref[...])
        ch, cm, cl = hi[:, h:h + 1], mid[:, h:h + 1], lo[:, h:h + 1]
        ones = jnp.where(lane < 6, 1.0, 0.0)
        q_ext = jnp.where(lane == 0, ch, jnp.where(lane == 1, cm, jnp.where(lane == 2, cl, ones)))
        k_ext = jnp.where(lane == 3, -ch, jnp.where(lane == 4, -cm, jnp.where(lane == 5, -cl, ones)))
        qo_ref[h, :, 0:HEAD_DIM] = qn.astype(BF16)
        qo_ref[h, :, HEAD_DIM:2 * HEAD_DIM] = q_ext.astype(BF16)
        ko_ref[h, :, 0:HEAD_DIM] = kn.astype(BF16)
        ko_ref[h, :, HEAD_DIM:2 * HEAD_DIM] = k_ext.astype(BF16)


def _qk_prep(z3, c3, q_g, k_g):
    b, s, _ = z3.shape
    ts = _tile(s, 512)
    blk = (ATT_HEADS // 2) * HEAD_DIM
    qb0, kb0 = OFF_Q // blk, OFF_K // blk
    assert OFF_Q % blk == 0 and OFF_K % blk == 0
    out = jax.ShapeDtypeStruct((b, ATT_HEADS, s, 2 * HEAD_DIM), BF16)
    zspec = lambda cb: pl.BlockSpec((None, ts, blk), lambda i, j, cb=cb: (i, j, cb))
    ospec = pl.BlockSpec((None, ATT_HEADS, ts, 2 * HEAD_DIM), lambda i, j: (i, 0, j, 0))
    return pl.pallas_call(
        _qk_prep_kernel,
        grid=(b, s // ts),
        in_specs=[zspec(qb0), zspec(qb0 + 1), zspec(kb0), zspec(kb0 + 1),
                  pl.BlockSpec((None, ts, LANES), lambda i, j: (i, j, 0)),
                  pl.BlockSpec((1, HEAD_DIM), lambda i, j: (0, 0)),
                  pl.BlockSpec((1, HEAD_DIM), lambda i, j: (0, 0))],
        out_specs=[ospec, ospec],
        out_shape=[out, out],
        compiler_params=_params("parallel", "parallel"),
        name="qk_prep",
    )(z3, z3, z3, z3, c3, q_g, k_g)


def _attn_kernel(q_ref, k_ref, v_ref, o_ref, *, tq):
    s_len = q_ref.shape[0]
    for i in range(s_len // tq):
        n = (i + 1) * tq
        q = q_ref[i * tq:(i + 1) * tq, :]
        s = lax.dot_general(q, k_ref[0:n, :], (((1,), (1,)), ((), ())), preferred_element_type=F32)
        row = lax.broadcasted_iota(jnp.int32, s.shape, 0) + i * tq
        col = lax.broadcasted_iota(jnp.int32, s.shape, 1)
        s = jnp.where(col <= row, s, NEG_BIG)
        m = jnp.max(s, axis=-1, keepdims=True)
        p = jnp.exp(s - m)
        l = jnp.sum(p, axis=-1, keepdims=True)
        acc = jnp.dot(p.astype(BF16), v_ref[0:n, :], preferred_element_type=F32)
        o_ref[i * tq:(i + 1) * tq, :] = (acc / l).astype(o_ref.dtype)


def _attention(qp, kp, z3):
    b, h, s, dk = qp.shape
    tq = _tile(s, 512)
    v0 = OFF_V // HEAD_DIM
    return pl.pallas_call(
        functools.partial(_attn_kernel, tq=tq),
        grid=(b, h),
        in_specs=[pl.BlockSpec((None, None, s, dk), lambda bi, hi: (bi, hi, 0, 0)),
                  pl.BlockSpec((None, None, s, dk), lambda bi, hi: (bi, hi, 0, 0)),
                  pl.BlockSpec((None, s, HEAD_DIM), lambda bi, hi: (bi, 0, v0 + hi))],
        out_specs=pl.BlockSpec((None, s, HEAD_DIM), lambda bi, hi: (bi, 0, hi)),
        out_shape=jax.ShapeDtypeStruct((b, s, ATT_WIDTH), BF16),
        compiler_params=_params("parallel", "parallel"),
        name="fox_attention",
    )(qp, kp, z3)


def _merge_kernel(ya_ref, yb_ref, yc_ref, wa_ref, wb_ref, wc_ref, g0_ref, g1_ref, g2_ref, o_ref):
    a = jnp.dot(ya_ref[...], wa_ref[...], preferred_element_type=F32)
    b = jnp.dot(yb_ref[...], wb_ref[...], preferred_element_type=F32)
    c = jnp.dot(yc_ref[...], wc_ref[...], preferred_element_type=F32)
    merged = g0_ref[...].astype(F32) * a + g1_ref[...].astype(F32) * b + g2_ref[...].astype(F32) * c
    o_ref[...] = merged.astype(o_ref.dtype)


def _merge(y_ab, y_c, w_br, gates, *, tm=1024, tn=1024):
    m = y_ab.shape[0]
    d = w_br.shape[1]
    tm, tn = _tile(m, tm), _tile(d, tn)
    nj = d // tn
    return pl.pallas_call(
        _merge_kernel,
        grid=(m // tm, nj),
        in_specs=[pl.BlockSpec((tm, CONV_WIDTH), lambda i, j: (i, 0)),
                  pl.BlockSpec((tm, POOL_WIDTH), lambda i, j: (i, 1)),
                  pl.BlockSpec((tm, ATT_WIDTH), lambda i, j: (i, 0)),
                  pl.BlockSpec((CONV_WIDTH, tn), lambda i, j: (0, j)),
                  pl.BlockSpec((POOL_WIDTH, tn), lambda i, j: (1, j)),
                  pl.BlockSpec((ATT_WIDTH, tn), lambda i, j: (1, j)),
                  pl.BlockSpec((tm, tn), lambda i, j: (i, j)),
                  pl.BlockSpec((tm, tn), lambda i, j: (i, nj + j)),
                  pl.BlockSpec((tm, tn), lambda i, j: (i, 2 * nj + j))],
        out_specs=pl.BlockSpec((tm, tn), lambda i, j: (i, j)),
        out_shape=jax.ShapeDtypeStruct((m, d), BF16),
        compiler_params=_params("parallel", "arbitrary"),
        name="branch_merge",
    )(y_ab, y_ab, y_c, w_br, w_br, w_br, gates, gates, gates)


def _out_proj_kernel(a_ref, w_ref, x_ref, o_ref):
    o_ref[...] = x_ref[...] + jnp.dot(a_ref[...], w_ref[...], preferred_element_type=F32)


def _out_proj(a, w, x2, *, tm=1024, tn=1024):
    m, k = a.shape
    n = w.shape[1]
    tm, tn = _tile(m, tm), _tile(n, tn)
    return pl.pallas_call(
        _out_proj_kernel,
        grid=(m // tm, n // tn),
        in_specs=[pl.BlockSpec((tm, k), lambda i, j: (i, 0)),
                  pl.BlockSpec((k, tn), lambda i, j: (0, j)),
                  pl.BlockSpec((tm, tn), lambda i, j: (i, j))],
        out_specs=pl.BlockSpec((tm, tn), lambda i, j: (i, j)),
        out_shape=jax.ShapeDtypeStruct((m, n), F32),
        compiler_params=_params("parallel", "arbitrary"),
        name="out_proj_residual",
    )(a, w, x2)


def _swiglu_partial(h, wg, wu, wd):
    a = jnp.dot(h, wg, preferred_element_type=F32)
    u = jnp.dot(h, wu, preferred_element_type=F32)
    act = (a * _sigmoid(a) * u).astype(BF16)
    return jnp.dot(act, wd, preferred_element_type=F32)


def _ffn_kernel(x_ref, g_ref, wg_ref, wu_ref, wd_ref, o_ref, h_ref, acc_ref):
    f = pl.program_id(1)

    @pl.when(f == 0)
    def _():
        h_ref[...] = _rmsnorm_rows(x_ref[...], g_ref[...]).astype(BF16)
        acc_ref[...] = jnp.zeros(acc_ref.shape, F32)

    acc_ref[...] += _swiglu_partial(h_ref[...], wg_ref[...], wu_ref[...], wd_ref[...])

    @pl.when(f == pl.num_programs(1) - 1)
    def _():
        o_ref[...] = x_ref[...] + acc_ref[...]


def _ffn(x2, g, wg, wu, wd, *, tm=512, tf=512):
    m, d = x2.shape
    ff = wg.shape[1]
    tm, tf = _tile(m, tm), _tile(ff, tf)
    return pl.pallas_call(
        _ffn_kernel,
        grid=(m // tm, ff // tf),
        in_specs=[pl.BlockSpec((tm, d), lambda i, f: (i, 0)),
                  pl.BlockSpec((1, d), lambda i, f: (0, 0)),
                  pl.BlockSpec((d, tf), lambda i, f: (0, f)),
                  pl.BlockSpec((d, tf), lambda i, f: (0, f)),
                  pl.BlockSpec((tf, d), lambda i, f: (f, 0))],
        out_specs=pl.BlockSpec((tm, d), lambda i, f: (i, 0)),
        out_shape=jax.ShapeDtypeStruct((m, d), F32),
        scratch_shapes=[pltpu.VMEM((tm, d), BF16), pltpu.VMEM((tm, d), F32)],
        compiler_params=_params("parallel", "arbitrary"),
        name="swiglu_dense",
    )(x2, g, wg, wu, wd)


def _moe_ffn_kernel(te_ref, nu_ref, tok_ref, x_ref, g_ref, wg_ref, wu_ref, wd_ref, o_ref,
                    xbuf_ref, h_ref, acc_ref, sem, *, tm):
    i = pl.program_id(0)
    f = pl.program_id(1)
    last = pl.num_programs(1) - 1
    n_used = nu_ref[0]
    used = i < n_used
    slot = i % 2

    def row_copy(tile, r, slot_):
        return pltpu.make_async_copy(x_ref.at[pl.ds(tok_ref[tile * tm + r], 1)],
                                     xbuf_ref.at[slot_, pl.ds(r, 1)], sem.at[slot_])

    def fetch_tile(tile, slot_):
        def body(r, c):
            row_copy(tile, r, slot_).start()
            return c
        lax.fori_loop(0, tm, body, 0, unroll=8)

    @pl.when((i == 0) & (f == 0) & used)
    def _():
        fetch_tile(0, 0)

    @pl.when((f == 0) & (i + 1 < n_used))
    def _():
        fetch_tile(i + 1, 1 - slot)

    @pl.when(used & (f == 0))
    def _():
        def body(r, c):
            row_copy(i, r, slot).wait()
            return c
        lax.fori_loop(0, tm, body, 0, unroll=8)
        h_ref[...] = _rmsnorm_rows(xbuf_ref[slot], g_ref[...]).astype(BF16)
        acc_ref[...] = jnp.zeros(acc_ref.shape, F32)

    @pl.when(used)
    def _():
        acc_ref[...] += _swiglu_partial(h_ref[...], wg_ref[...], wu_ref[...], wd_ref[...])

    @pl.when(f == last)
    def _():
        o_ref[...] = jnp.where(used, acc_ref[...], 0.0)


def _moe_ffn(tile_expert, n_used, src_tok, x2, g, wg, wu, wd, *, tm, tf=512):
    mp = src_tok.shape[0]
    d = x2.shape[1]
    ff = wg.shape[2]
    tf = _tile(ff, tf)
    nf = ff // tf

    def wspec(shape, fdim):
        def imap(i, f, te, nu, tok):
            fi = jnp.where(i < nu[0], f, nf - 1)
            return (te[i], fi, 0) if fdim == 1 else (te[i], 0, fi)
        return pl.BlockSpec(shape, imap)

    return pl.pallas_call(
        functools.partial(_moe_ffn_kernel, tm=tm),
        grid_spec=pltpu.PrefetchScalarGridSpec(
            num_scalar_prefetch=3,
            grid=(mp // tm, nf),
            in_specs=[pl.BlockSpec(memory_space=pl.ANY),
                      pl.BlockSpec((1, d), lambda i, f, te, nu, tok: (0, 0)),
                      wspec((None, d, tf), 2),
                      wspec((None, d, tf), 2),
                      wspec((None, tf, d), 1)],
            out_specs=pl.BlockSpec((tm, d), lambda i, f, te, nu, tok: (i, 0)),
            scratch_shapes=[pltpu.VMEM((2, tm, d), F32), pltpu.VMEM((tm, d), BF16), pltpu.VMEM((tm, d), F32),
                            pltpu.SemaphoreType.DMA((2,))]),
        out_shape=jax.ShapeDtypeStruct((mp, d), F32),
        compiler_params=_params("arbitrary", "arbitrary"),
        name="swiglu_experts_grouped",
    )(tile_expert, n_used, src_tok, x2, g, wg, wu, wd)


def _combine_kernel(p1_ref, p2_ref, x_ref, sel_ref, y_ref, o_ref, buf_ref, sem, *, tc):
    base = pl.program_id(0) * tc

    def row_copy(src_row, slot, r):
        return pltpu.make_async_copy(y_ref.at[pl.ds(src_row, 1)], buf_ref.at[slot, pl.ds(r, 1)], sem)

    def issue(r, c):
        row_copy(p1_ref[base + r], 0, r).start()
        row_copy(p2_ref[base + r], 1, r).start()
        return c

    def drain(r, c):
        row_copy(0, 0, r).wait()
        row_copy(0, 1, r).wait()
        return c

    lax.fori_loop(0, tc, issue, 0, unroll=8)
    lax.fori_loop(0, tc, drain, 0, unroll=8)
    sel = sel_ref[...]
    w1, w2 = sel[:, TOP_K:TOP_K + 1], sel[:, TOP_K + 1:TOP_K + 2]
    o_ref[...] = x_ref[...] + w1 * buf_ref[0] + w2 * buf_ref[1]


def _combine(p1, p2, x2, sel, y, *, tc=512):
    m, d = x2.shape
    tc = _tile(m, tc)
    return pl.pallas_call(
        functools.partial(_combine_kernel, tc=tc),
        grid_spec=pltpu.PrefetchScalarGridSpec(
            num_scalar_prefetch=2,
            grid=(m // tc,),
            in_specs=[pl.BlockSpec((tc, d), lambda i, p1, p2: (i, 0)),
                      pl.BlockSpec((tc, LANES), lambda i, p1, p2: (i, 0)),
                      pl.BlockSpec(memory_space=pl.ANY)],
            out_specs=pl.BlockSpec((tc, d), lambda i, p1, p2: (i, 0)),
            scratch_shapes=[pltpu.VMEM((TOP_K, tc, d), F32), pltpu.SemaphoreType.DMA(())]),
        out_shape=jax.ShapeDtypeStruct((m, d), F32),
        compiler_params=_params("arbitrary"),
        name="combine_expert_rows",
    )(p1, p2, x2, sel, y)


def _route(sel, tm):
    m = sel.shape[0]
    e_flat = sel[:, :TOP_K].astype(jnp.int32).T.reshape(-1)
    onehot = (e_flat[:, None] == jnp.arange(N_EXPERTS, dtype=jnp.int32)[None, :]).astype(jnp.int32)
    csum = jnp.cumsum(onehot, axis=0)
    rank = jnp.sum((csum - onehot) * onehot, axis=1)
    counts = csum[-1]
    padded = ((counts + tm - 1) // tm) * tm
    ends = jnp.cumsum(padded)
    dest = (ends - padded)[e_flat] + rank
    mp = TOP_K * m + N_EXPERTS * tm
    n_tiles = mp // tm
    tok = jnp.tile(jnp.arange(m, dtype=jnp.int32), TOP_K)
    src_tok = jnp.zeros((mp,), jnp.int32).at[dest].set(tok)
    n_used = (ends[-1] // tm).astype(jnp.int32)
    tile_id = jnp.arange(n_tiles, dtype=jnp.int32)
    tile_e = jnp.sum((ends[None, :] <= (tile_id * tm)[:, None]).astype(jnp.int32), axis=1)
    tile_e = jnp.minimum(tile_e, N_EXPERTS - 1)
    tile_e = jnp.where(tile_id < n_used, tile_e, tile_e[jnp.maximum(n_used - 1, 0)])
    return src_tok, tile_e, n_used.reshape(1), dest[:m], dest[m:]


def _moe(x2, g, router_w, wg, wu, wd):
    m, d = x2.shape
    tm = _tile(m, MOE_ROW_TILE)
    sel = _norm_proj(x2, g, _pad_cols(router_w, LANES), jnp.zeros((1, LANES), F32),
                     act="top2", precise=True, out_dtype=F32, tm=1024, tn=LANES)
    src_tok, tile_e, n_used, p1, p2 = _route(sel, tm)
    y = _moe_ffn(tile_e, n_used, src_tok, x2, g, wg, wu, wd, tm=tm)
    return _combine(p1, p2, x2, sel, y)


def _pad_cols(a, n):
    return jnp.pad(a, ((0, 0), (0, n - a.shape[1])))


def _mixing_sublayer(x2, bsz, seq, norm_g, w_in, b_f, b_gate, conv_w, conv_b, cln_g, cln_b,
                     pool_w, pool_scale, q_g, k_g, w_br, w_o):
    row = lambda v: v.reshape(1, -1)
    w_main = w_in[:, :OFF_F].astype(BF16)
    w_f = _pad_cols(w_in[:, OFF_F:OFF_G], LANES).astype(BF16)
    w_gate = w_in[:, OFF_G:].astype(BF16)

    z, f_logit = _in_proj(x2, row(norm_g), w_main, w_f, _pad_cols(row(b_f), LANES))
    gates = _norm_proj(x2, row(norm_g), w_gate, row(b_gate), act="sigmoid", out_dtype=BF16, tm=1024, tn=2048)

    z3 = z.reshape(bsz, seq, OFF_F)
    c3 = _log_forget_cumsum(f_logit.reshape(bsz, seq, LANES))
    y_ab = _mixers(z3, conv_w, row(conv_b), row(cln_g), row(cln_b), pool_w.astype(BF16), row(pool_scale))
    qp, kp = _qk_prep(z3, c3, row(q_g), row(k_g))
    y_c = _attention(qp, kp, z3)

    merged = _merge(y_ab.reshape(bsz * seq, -1), y_c.reshape(bsz * seq, -1), w_br.astype(BF16), gates)
    return _out_proj(merged, w_o.astype(BF16), x2)


def kernel(x, norm1_g, w_in, b_f, b_gate, conv_w, conv_b, cln_g, cln_b, pool_w, pool_scale, q_g, k_g, w_br, w_o,
           norm2_g, ffn_wg, ffn_wu, ffn_wd, router, exp_wg, exp_wu, exp_wd):
    bsz, seq, d = x.shape
    depth = w_in.shape[0]
    x2 = x.reshape(bsz * seq, d)
    for l in range(depth):
        x2 = _mixing_sublayer(x2, bsz, seq, norm1_g[l], w_in[l], b_f[l], b_gate[l], conv_w[l], conv_b[l],
                              cln_g[l], cln_b[l], pool_w[l], pool_scale[l], q_g[l], k_g[l], w_br[l], w_o[l])
        g2 = norm2_g[l].reshape(1, d)
        j = l // 2
        if l % 2 == 0:
            x2 = _ffn(x2, g2, ffn_wg[j].astype(BF16), ffn_wu[j].astype(BF16), ffn_wd[j].astype(BF16))
        else:
            x2 = _moe(x2, g2, router[j], exp_wg[j].astype(BF16), exp_wu[j].astype(BF16), exp_wd[j].astype(BF16))
    return x2.reshape(bsz, seq, d)
```

```python
import functools

import jax
import jax.numpy as jnp
from jax import lax
from jax.experimental import pallas as pl
from jax.experimental.pallas import tpu as pltpu

EPS = 1e-6
CONV_WIDTH = 512
CONV_K = 31
POOL_WIDTH = 512
POOL_WINDOWS = (2, 4, 8, 16)
POOL_GROUP = POOL_WIDTH // len(POOL_WINDOWS)
ATT_HEADS = 8
HEAD_DIM = 128
ATT_WIDTH = ATT_HEADS * HEAD_DIM
N_BRANCH = 3
N_EXPERTS = 8
TOP_K = 2

OFF_B = 2 * CONV_WIDTH
OFF_Q = OFF_B + POOL_WIDTH
OFF_K = OFF_Q + ATT_WIDTH
OFF_V = OFF_K + ATT_WIDTH
OFF_F = OFF_V + ATT_WIDTH
OFF_G = OFF_F + ATT_HEADS

LANES = 128
SUBLANES = 8
HALO = 32
NEG_BIG = -1e30
VMEM_LIMIT_BYTES = 56 * 1024 * 1024
MOE_ROW_TILE = 512

F32 = jnp.float32
BF16 = jnp.bfloat16


def _tile(n, pref):
    t = min(n, pref)
    while n % t:
        t -= 1
    return t


def _params(*sem):
    return pltpu.CompilerParams(dimension_semantics=sem, vmem_limit_bytes=VMEM_LIMIT_BYTES)


def _rmsnorm_rows(x, g):
    ms = jnp.mean(x * x, axis=-1, keepdims=True)
    return x * lax.rsqrt(ms + EPS) * g


def _sigmoid(x):
    return 1.0 / (1.0 + jnp.exp(-x))


def _log_sigmoid(x):
    return jnp.minimum(x, 0.0) - jnp.log1p(jnp.exp(-jnp.abs(x)))


def _top2(logits):
    lane = lax.broadcasted_iota(jnp.int32, logits.shape, 1).astype(F32)
    lg = jnp.where(lane < N_EXPERTS, logits, NEG_BIG)
    m1 = jnp.max(lg, axis=-1, keepdims=True)
    i1 = jnp.min(jnp.where(lg == m1, lane, float(LANES)), axis=-1, keepdims=True)
    lg2 = jnp.where(lane == i1, NEG_BIG, lg)
    m2 = jnp.max(lg2, axis=-1, keepdims=True)
    i2 = jnp.min(jnp.where(lg2 == m2, lane, float(LANES)), axis=-1, keepdims=True)
    e = jnp.exp(m2 - m1)
    w1 = 1.0 / (1.0 + e)
    w2 = e / (1.0 + e)
    return jnp.where(lane == 0, i1, jnp.where(lane == 1, i2, jnp.where(lane == 2, w1, jnp.where(lane == 3, w2, 0.0))))


def _split2(a):
    hi = a.astype(BF16)
    return hi, (a - hi.astype(F32)).astype(BF16)


def _router_kernel(x_ref, g_ref, whi_ref, wlo_ref, o_ref):
    h_hi, h_lo = _split2(_rmsnorm_rows(x_ref[...], g_ref[...]))
    logits = (jnp.dot(h_hi, whi_ref[...], preferred_element_type=F32)
              + jnp.dot(h_lo, whi_ref[...], preferred_element_type=F32)
              + jnp.dot(h_hi, wlo_ref[...], preferred_element_type=F32))
    o_ref[...] = _top2(logits)


def _router(x2, g, router_w, *, tm=1024):
    m, d = x2.shape
    tm = _tile(m, tm)
    w_hi, w_lo = _split2(_pad_cols(router_w, LANES))
    return pl.pallas_call(
        _router_kernel,
        grid=(m // tm,),
        in_specs=[pl.BlockSpec((tm, d), lambda i: (i, 0)),
                  pl.BlockSpec((1, d), lambda i: (0, 0)),
                  pl.BlockSpec((d, LANES), lambda i: (0, 0)),
                  pl.BlockSpec((d, LANES), lambda i: (0, 0))],
        out_specs=pl.BlockSpec((tm, LANES), lambda i: (i, 0)),
        out_shape=jax.ShapeDtypeStruct((m, LANES), F32),
        compiler_params=_params("parallel"),
        name="router_top2",
    )(x2, g, w_hi, w_lo)


def _in_proj_kernel(x_ref, g_ref, w_ref, wf_ref, bf_ref, o_ref, f_ref, h_ref):
    @pl.when(pl.program_id(1) == 0)
    def _():
        h = _rmsnorm_rows(x_ref[...], g_ref[...]).astype(BF16)
        h_ref[...] = h
        f_ref[...] = jnp.dot(h, wf_ref[...], preferred_element_type=F32) + bf_ref[...]

    o_ref[...] = jnp.dot(h_ref[...], w_ref[...], preferred_element_type=F32).astype(o_ref.dtype)


def _in_proj(x2, g, w, w_f, b_f, *, tm=1024, tn=1536):
    m, d = x2.shape
    n = w.shape[1]
    tm, tn = _tile(m, tm), _tile(n, tn)
    return pl.pallas_call(
        _in_proj_kernel,
        grid=(m // tm, n // tn),
        in_specs=[pl.BlockSpec((tm, d), lambda i, j: (i, 0)),
                  pl.BlockSpec((1, d), lambda i, j: (0, 0)),
                  pl.BlockSpec((d, tn), lambda i, j: (0, j)),
                  pl.BlockSpec((d, LANES), lambda i, j: (0, 0)),
                  pl.BlockSpec((1, LANES), lambda i, j: (0, 0))],
        out_specs=[pl.BlockSpec((tm, tn), lambda i, j: (i, j)),
                   pl.BlockSpec((tm, LANES), lambda i, j: (i, 0)),
                   pl.BlockSpec((tm, d), lambda i, j: (i, 0))],
        out_shape=[jax.ShapeDtypeStruct((m, n), BF16), jax.ShapeDtypeStruct((m, LANES), F32),
                   jax.ShapeDtypeStruct((m, d), BF16)],
        compiler_params=_params("parallel", "arbitrary"),
        name="in_proj",
    )(x2, g, w, w_f, b_f)


def _cumsum_kernel(f_ref, o_ref, *, chunk):
    s = f_ref.shape[0]
    row = lax.broadcasted_iota(jnp.int32, (chunk, chunk), 0)
    col = lax.broadcasted_iota(jnp.int32, (chunk, chunk), 1)
    tril = (row >= col).astype(F32)
    carry = jnp.zeros((1, f_ref.shape[1]), F32)
    for c in range(s // chunk):
        ls = _log_sigmoid(f_ref[c * chunk:(c + 1) * chunk, :])
        cs = jnp.dot(tril, ls, preferred_element_type=F32, precision=lax.Precision.HIGHEST) + carry
        o_ref[c * chunk:(c + 1) * chunk, :] = cs
        carry = cs[chunk - 1:chunk, :]


def _log_forget_cumsum(f3):
    b, s, n = f3.shape
    chunk = _tile(s, LANES)
    return pl.pallas_call(
        functools.partial(_cumsum_kernel, chunk=chunk),
        grid=(b,),
        in_specs=[pl.BlockSpec((None, s, n), lambda i: (i, 0, 0))],
        out_specs=pl.BlockSpec((None, s, n), lambda i: (i, 0, 0)),
        out_shape=jax.ShapeDtypeStruct((b, s, n), F32),
        compiler_params=_params("parallel"),
        name="log_forget_cumsum",
    )(f3)


def _shift_rows(win, p):
    if p == 0:
        return win
    return pltpu.roll(win, win.shape[0] - p, axis=0)


def _mixers_kernel(z_ref, cw_ref, cb_ref, lg_ref, lb_ref, pw_ref, ps_ref, o_ref, zp_ref, up_ref, cv_ref, *, rows):
    s = z_ref.shape[0]
    n_chunks = s // rows
    zp_ref[0:HALO, :] = jnp.zeros((HALO, CONV_WIDTH), F32)
    up_ref[0:HALO, :] = jnp.zeros((HALO, POOL_WIDTH), F32)

    def fill(c, _):
        r0 = pl.multiple_of(c * rows, rows)
        a = z_ref[pl.ds(r0, rows), 0:CONV_WIDTH].astype(F32)
        gt = z_ref[pl.ds(r0, rows), CONV_WIDTH:OFF_B].astype(F32)
        zp_ref[pl.ds(r0 + HALO, rows), :] = a * _sigmoid(gt)
        up_ref[pl.ds(r0 + HALO, rows), :] = z_ref[pl.ds(r0, rows), OFF_B:OFF_Q].astype(F32)
        return 0

    lax.fori_loop(0, n_chunks, fill, 0)

    lead = HALO - (CONV_K - 1)

    def mix(c, _):
        r0 = pl.multiple_of(c * rows, rows)
        for lg in range(CONV_WIDTH // LANES):
            ls = slice(lg * LANES, (lg + 1) * LANES)
            win = zp_ref[pl.ds(r0, rows + HALO), ls]
            acc = jnp.zeros((rows, LANES), F32) + cb_ref[:, ls]
            for p in range(SUBLANES):
                wp = _shift_rows(win, p)
                for k in range(CONV_K):
                    if (lead + k) % SUBLANES == p:
                        off = lead + k - p
                        acc = acc + cw_ref[k:k + 1, ls] * wp[off:off + rows, :]
            cv_ref[:, ls] = acc
        y = cv_ref[...]
        mu = jnp.mean(y, axis=-1, keepdims=True)
        yc = y - mu
        var = jnp.mean(yc * yc, axis=-1, keepdims=True)
        y = yc * lax.rsqrt(var + EPS) * lg_ref[...] + lb_ref[...]
        o_ref[pl.ds(r0, rows), 0:CONV_WIDTH] = (y * _sigmoid(y)).astype(o_ref.dtype)

        half = HALO // 2
        pos = (lax.broadcasted_iota(jnp.int32, (rows, LANES), 0) + (r0 + 1)).astype(F32)
        for gi, w in enumerate(POOL_WINDOWS):
            ls = slice(gi * POOL_GROUP, (gi + 1) * POOL_GROUP)
            win = up_ref[pl.ds(r0 + half, rows + half), ls]
            tot = win
            span = 1
            while span < w:
                tot = tot + pltpu.roll(tot, span, axis=0)
                span *= 2
            u = win[half:half + rows, :]
            p = tot[half:half + rows, :] / jnp.minimum(pos, float(w)) - u
            yb = jnp.dot(p.astype(BF16), pw_ref[gi], preferred_element_type=F32) * ps_ref[:, ls]
            o_ref[pl.ds(r0, rows), CONV_WIDTH + gi * POOL_GROUP:CONV_WIDTH + (gi + 1) * POOL_GROUP] = yb.astype(o_ref.dtype)
        return 0

    lax.fori_loop(0, n_chunks, mix, 0)


def _mixers(z3, conv_w, conv_b, cln_g, cln_b, pool_w, pool_scale):
    b, s, n = z3.shape
    rows = _tile(s, 256)
    width = OFF_Q
    assert n % width == 0
    return pl.pallas_call(
        functools.partial(_mixers_kernel, rows=rows),
        grid=(b,),
        in_specs=[pl.BlockSpec((None, s, width), lambda i: (i, 0, 0)),
                  pl.BlockSpec(conv_w.shape, lambda i: (0, 0)),
                  pl.BlockSpec(conv_b.shape, lambda i: (0, 0)),
                  pl.BlockSpec(cln_g.shape, lambda i: (0, 0)),
                  pl.BlockSpec(cln_b.shape, lambda i: (0, 0)),
                  pl.BlockSpec(pool_w.shape, lambda i: (0, 0, 0)),
                  pl.BlockSpec(pool_scale.shape, lambda i: (0, 0))],
        out_specs=pl.BlockSpec((None, s, CONV_WIDTH + POOL_WIDTH), lambda i: (i, 0, 0)),
        out_shape=jax.ShapeDtypeStruct((b, s, CONV_WIDTH + POOL_WIDTH), BF16),
        scratch_shapes=[pltpu.VMEM((s + HALO, CONV_WIDTH), F32),
                        pltpu.VMEM((s + HALO, POOL_WIDTH), F32),
                        pltpu.VMEM((rows, CONV_WIDTH), F32)],
        compiler_params=_params("parallel"),
        name="conv_pool_mixers",
    )(z3, conv_w, conv_b, cln_g, cln_b, pool_w, pool_scale)


def _split3(c):
    hi = c.astype(BF16).astype(F32)
    r1 = c - hi
    mid = r1.astype(BF16).astype(F32)
    return hi, mid, r1 - mid


def _qk_prep_kernel(qa_ref, qb_ref, ka_ref, kb_ref, c_ref, qg_ref, kg_ref, qo_ref, ko_ref):
    ts = c_ref.shape[0]
    scale = HEAD_DIM ** -0.5
    hi, mid, lo = _split3(c_ref[...])
    lane = lax.broadcasted_iota(jnp.int32, (ts, LANES), 1)
    half = ATT_HEADS // 2
    for h in range(ATT_HEADS):
        q_src, k_src = (qa_ref, ka_ref) if h < half else (qb_ref, kb_ref)
        ls = slice((h % half) * HEAD_DIM, (h % half + 1) * HEAD_DIM)
        qn = _rmsnorm_rows(q_src[:, ls].astype(F32), qg_ref[...]) * scale
        kn = _rmsnorm_rows(k_src[:, ls].astype(F32), kg_ref[...])
        ch, cm, cl = hi[:, h:h + 1], mid[:, h:h + 1], lo[:, h:h + 1]
        ones = jnp.where(lane < 6, 1.0, 0.0)
        q_ext = jnp.where(lane == 0, ch, jnp.where(lane == 1, cm, jnp.where(lane == 2, cl, ones)))
        k_ext = jnp.where(lane == 3, -ch, jnp.where(lane == 4, -cm, jnp.where(lane == 5, -cl, ones)))
        qo_ref[h, :, 0:HEAD_DIM] = qn.astype(BF16)
        qo_ref[h, :, HEAD_DIM:2 * HEAD_DIM] = q_ext.astype(BF16)
        ko_ref[h, :, 0:HEAD_DIM] = kn.astype(BF16)
        ko_ref[h, :, HEAD_DIM:2 * HEAD_DIM] = k_ext.astype(BF16)


def _qk_prep(z3, c3, q_g, k_g):
    b, s, _ = z3.shape
    ts = _tile(s, 512)
    blk = (ATT_HEADS // 2) * HEAD_DIM
    qb0, kb0 = OFF_Q // blk, OFF_K // blk
    assert OFF_Q % blk == 0 and OFF_K % blk == 0
    out = jax.ShapeDtypeStruct((b, ATT_HEADS, s, 2 * HEAD_DIM), BF16)
    zspec = lambda cb: pl.BlockSpec((None, ts, blk), lambda i, j, cb=cb: (i, j, cb))
    ospec = pl.BlockSpec((None, ATT_HEADS, ts, 2 * HEAD_DIM), lambda i, j: (i, 0, j, 0))
    return pl.pallas_call(
        _qk_prep_kernel,
        grid=(b, s // ts),
        in_specs=[zspec(qb0), zspec(qb0 + 1), zspec(kb0), zspec(kb0 + 1),
                  pl.BlockSpec((None, ts, LANES), lambda i, j: (i, j, 0)),
                  pl.BlockSpec((1, HEAD_DIM), lambda i, j: (0, 0)),
                  pl.BlockSpec((1, HEAD_DIM), lambda i, j: (0, 0))],
        out_specs=[ospec, ospec],
        out_shape=[out, out],
        compiler_params=_params("parallel", "parallel"),
        name="qk_prep",
    )(z3, z3, z3, z3, c3, q_g, k_g)


ATT_HEADS_PER_STEP = 2


def _attn_kernel(q_ref, k_ref, v_ref, o_ref, *, tq):
    heads, s_len, _ = q_ref.shape
    for hh in range(heads):
        ls = slice(hh * HEAD_DIM, (hh + 1) * HEAD_DIM)
        for i in range(s_len // tq):
            n = (i + 1) * tq
            q = q_ref[hh, i * tq:(i + 1) * tq, :]
            s = lax.dot_general(q, k_ref[hh, 0:n, :], (((1,), (1,)), ((), ())), preferred_element_type=F32)
            row = lax.broadcasted_iota(jnp.int32, s.shape, 0) + i * tq
            col = lax.broadcasted_iota(jnp.int32, s.shape, 1)
            s = jnp.where(col <= row, s, NEG_BIG)
            m = jnp.max(s, axis=-1, keepdims=True)
            p = jnp.exp(s - m)
            l = jnp.sum(p, axis=-1, keepdims=True)
            acc = jnp.dot(p.astype(BF16), v_ref[0:n, ls], preferred_element_type=F32)
            o_ref[i * tq:(i + 1) * tq, ls] = (acc / l).astype(o_ref.dtype)


def _attention(qp, kp, z3):
    b, h, s, dk = qp.shape
    tq = _tile(s, 512)
    hs = ATT_HEADS_PER_STEP
    width = hs * HEAD_DIM
    assert h % hs == 0 and OFF_V % width == 0
    v0 = OFF_V // width
    return pl.pallas_call(
        functools.partial(_attn_kernel, tq=tq),
        grid=(b, h // hs),
        in_specs=[pl.BlockSpec((None, hs, s, dk), lambda bi, hi: (bi, hi, 0, 0)),
                  pl.BlockSpec((None, hs, s, dk), lambda bi, hi: (bi, hi, 0, 0)),
                  pl.BlockSpec((None, s, width), lambda bi, hi: (bi, 0, v0 + hi))],
        out_specs=pl.BlockSpec((None, s, width), lambda bi, hi: (bi, 0, hi)),
        out_shape=jax.ShapeDtypeStruct((b, s, ATT_WIDTH), BF16),
        compiler_params=_params("parallel", "parallel"),
        name="fox_attention",
    )(qp, kp, z3)


def _merge_kernel(h_ref, ya_ref, yb_ref, yc_ref, wa_ref, wb_ref, wc_ref,
                  wg0_ref, wg1_ref, wg2_ref, bg0_ref, bg1_ref, bg2_ref, o_ref):
    h = h_ref[...]

    def gated(y_ref, w_ref, wg_ref, bg_ref):
        gate = _sigmoid(jnp.dot(h, wg_ref[...], preferred_element_type=F32) + bg_ref[...])
        return gate * jnp.dot(y_ref[...], w_ref[...], preferred_element_type=F32)

    merged = (gated(ya_ref, wa_ref, wg0_ref, bg0_ref) + gated(yb_ref, wb_ref, wg1_ref, bg1_ref)
              + gated(yc_ref, wc_ref, wg2_ref, bg2_ref))
    o_ref[...] = merged.astype(o_ref.dtype)


def _merge(h, y_ab, y_c, w_br, w_gate, b_gate, *, tm=1024, tn=512):
    m, d = h.shape
    tm, tn = _tile(m, tm), _tile(d, tn)
    nj = d // tn
    gate_w = lambda br: pl.BlockSpec((d, tn), lambda i, j, br=br: (0, br * nj + j))
    gate_b = lambda br: pl.BlockSpec((1, tn), lambda i, j, br=br: (0, br * nj + j))
    return pl.pallas_call(
        _merge_kernel,
        grid=(m // tm, nj),
        in_specs=[pl.BlockSpec((tm, d), lambda i, j: (i, 0)),
                  pl.BlockSpec((tm, CONV_WIDTH), lambda i, j: (i, 0)),
                  pl.BlockSpec((tm, POOL_WIDTH), lambda i, j: (i, 1)),
                  pl.BlockSpec((tm, ATT_WIDTH), lambda i, j: (i, 0)),
                  pl.BlockSpec((CONV_WIDTH, tn), lambda i, j: (0, j)),
                  pl.BlockSpec((POOL_WIDTH, tn), lambda i, j: (1, j)),
                  pl.BlockSpec((ATT_WIDTH, tn), lambda i, j: (1, j)),
                  gate_w(0), gate_w(1), gate_w(2), gate_b(0), gate_b(1), gate_b(2)],
        out_specs=pl.BlockSpec((tm, tn), lambda i, j: (i, j)),
        out_shape=jax.ShapeDtypeStruct((m, d), BF16),
        compiler_params=_params("parallel", "arbitrary"),
        name="gated_branch_merge",
    )(h, y_ab, y_ab, y_c, w_br, w_br, w_br, w_gate, w_gate, w_gate, b_gate, b_gate, b_gate)


def _out_proj_kernel(a_ref, w_ref, x_ref, o_ref):
    o_ref[...] = x_ref[...] + jnp.dot(a_ref[...], w_ref[...], preferred_element_type=F32)


def _out_proj(a, w, x2, *, tm=1024, tn=1024):
    m, k = a.shape
    n = w.shape[1]
    tm, tn = _tile(m, tm), _tile(n, tn)
    return pl.pallas_call(
        _out_proj_kernel,
        grid=(m // tm, n // tn),
        in_specs=[pl.BlockSpec((tm, k), lambda i, j: (i, 0)),
                  pl.BlockSpec((k, tn), lambda i, j: (0, j)),
                  pl.BlockSpec((tm, tn), lambda i, j: (i, j))],
        out_specs=pl.BlockSpec((tm, tn), lambda i, j: (i, j)),
        out_shape=jax.ShapeDtypeStruct((m, n), F32),
        compiler_params=_params("parallel", "arbitrary"),
        name="out_proj_residual",
    )(a, w, x2)


def _swiglu_partial(h, wg, wu, wd):
    a = jnp.dot(h, wg, preferred_element_type=F32)
    u = jnp.dot(h, wu, preferred_element_type=F32)
    act = (a * _sigmoid(a) * u).astype(BF16)
    return jnp.dot(act, wd, preferred_element_type=F32)


def _ffn_kernel(x_ref, g_ref, wg_ref, wu_ref, wd_ref, o_ref, h_ref, acc_ref):
    f = pl.program_id(1)

    @pl.when(f == 0)
    def _():
        h_ref[...] = _rmsnorm_rows(x_ref[...], g_ref[...]).astype(BF16)
        acc_ref[...] = jnp.zeros(acc_ref.shape, F32)

    acc_ref[...] += _swiglu_partial(h_ref[...], wg_ref[...], wu_ref[...], wd_ref[...])

    @pl.when(f == pl.num_programs(1) - 1)
    def _():
        o_ref[...] = x_ref[...] + acc_ref[...]


def _ffn(x2, g, wg, wu, wd, *, tm=512, tf=512):
    m, d = x2.shape
    ff = wg.shape[1]
    tm, tf = _tile(m, tm), _tile(ff, tf)
    return pl.pallas_call(
        _ffn_kernel,
        grid=(m // tm, ff // tf),
        in_specs=[pl.BlockSpec((tm, d), lambda i, f: (i, 0)),
                  pl.BlockSpec((1, d), lambda i, f: (0, 0)),
                  pl.BlockSpec((d, tf), lambda i, f: (0, f)),
                  pl.BlockSpec((d, tf), lambda i, f: (0, f)),
                  pl.BlockSpec((tf, d), lambda i, f: (f, 0))],
        out_specs=pl.BlockSpec((tm, d), lambda i, f: (i, 0)),
        out_shape=jax.ShapeDtypeStruct((m, d), F32),
        scratch_shapes=[pltpu.VMEM((tm, d), BF16), pltpu.VMEM((tm, d), F32)],
        compiler_params=_params("parallel", "arbitrary"),
        name="swiglu_dense",
    )(x2, g, wg, wu, wd)


def _moe_ffn_kernel(te_ref, nu_ref, tok_ref, x_ref, g_ref, wg_ref, wu_ref, wd_ref, o_ref,
                    xbuf_ref, h_ref, acc_ref, sem, *, tm, rows_step):
    i = pl.program_id(0)
    f = pl.program_id(1)
    last = pl.num_programs(1) - 1
    n_used = nu_ref[0]
    used = i < n_used
    slot = i % 2
    rows_tile = xbuf_ref.shape[1]

    def row_copy(tile, r, slot_):
        return pltpu.make_async_copy(x_ref.at[pl.ds(tok_ref[tile * tm + r], 1)],
                                     xbuf_ref.at[slot_, pl.ds(r, 1)], sem.at[slot_])

    @pl.when((i == 0) & (f == 0))
    def _():
        def body(r, c):
            row_copy(0, r, 0).start()
            return c
        lax.fori_loop(0, rows_tile, body, 0, unroll=8)

    @pl.when((f == 0) & (i <= n_used))
    def _():
        def body(r, c):
            row_copy(i, r, slot).wait()
            return c
        lax.fori_loop(0, rows_tile, body, 0, unroll=8)

    @pl.when(used & (f == 0))
    def _():
        h_ref[...] = _rmsnorm_rows(xbuf_ref[slot, pl.ds(0, tm), :], g_ref[...]).astype(BF16)
        acc_ref[...] = jnp.zeros(acc_ref.shape, F32)

    @pl.when(used)
    def _():
        for r in range(rows_step):
            row_copy(i + 1, f * rows_step + r, 1 - slot).start()
        acc_ref[...] += _swiglu_partial(h_ref[...], wg_ref[...], wu_ref[...], wd_ref[...])

    @pl.when(f == last)
    def _():
        o_ref[...] = jnp.where(used, acc_ref[...], 0.0)


def _moe_ffn(tile_expert, n_used, src_tok, x2, g, wg, wu, wd, *, tm, tf=512):
    mp = src_tok.shape[0]
    d = x2.shape[1]
    ff = wg.shape[2]
    tf = _tile(ff, tf)
    nf = ff // tf
    rows_step = -(-tm // (nf * SUBLANES)) * SUBLANES
    rows_tile = rows_step * nf
    src_tok = jnp.pad(src_tok, (0, 2 * rows_tile))

    def wspec(shape, fdim):
        def imap(i, f, te, nu, tok):
            fi = jnp.where(i < nu[0], f, nf - 1)
            return (te[i], fi, 0) if fdim == 1 else (te[i], 0, fi)
        return pl.BlockSpec(shape, imap)

    return pl.pallas_call(
        functools.partial(_moe_ffn_kernel, tm=tm, rows_step=rows_step),
        grid_spec=pltpu.PrefetchScalarGridSpec(
            num_scalar_prefetch=3,
            grid=(mp // tm, nf),
            in_specs=[pl.BlockSpec(memory_space=pl.ANY),
                      pl.BlockSpec((1, d), lambda i, f, te, nu, tok: (0, 0)),
                      wspec((None, d, tf), 2),
                      wspec((None, d, tf), 2),
                      wspec((None, tf, d), 1)],
            out_specs=pl.BlockSpec((tm, d), lambda i, f, te, nu, tok: (i, 0)),
            scratch_shapes=[pltpu.VMEM((2, rows_tile, d), F32), pltpu.VMEM((tm, d), BF16),
                            pltpu.VMEM((tm, d), F32), pltpu.SemaphoreType.DMA((2,))]),
        out_shape=jax.ShapeDtypeStruct((mp, d), F32),
        compiler_params=_params("arbitrary", "arbitrary"),
        name="swiglu_experts_grouped",
    )(tile_expert, n_used, src_tok, x2, g, wg, wu, wd)


def _combine_kernel(p1_ref, p2_ref, x_ref, sel_ref, y_ref, o_ref, buf_ref, sem, *, tc):
    base = pl.program_id(0) * tc

    def row_copy(src_row, slot, r):
        return pltpu.make_async_copy(y_ref.at[pl.ds(src_row, 1)], buf_ref.at[slot, pl.ds(r, 1)], sem)

    def issue(r, c):
        row_copy(p1_ref[base + r], 0, r).start()
        row_copy(p2_ref[base + r], 1, r).start()
        return c

    def drain(r, c):
        row_copy(0, 0, r).wait()
        row_copy(0, 1, r).wait()
        return c

    lax.fori_loop(0, tc, issue, 0, unroll=8)
    lax.fori_loop(0, tc, drain, 0, unroll=8)
    sel = sel_ref[...]
    w1, w2 = sel[:, TOP_K:TOP_K + 1], sel[:, TOP_K + 1:TOP_K + 2]
    o_ref[...] = x_ref[...] + w1 * buf_ref[0] + w2 * buf_ref[1]


def _combine(p1, p2, x2, sel, y, *, tc=512):
    m, d = x2.shape
    tc = _tile(m, tc)
    return pl.pallas_call(
        functools.partial(_combine_kernel, tc=tc),
        grid_spec=pltpu.PrefetchScalarGridSpec(
            num_scalar_prefetch=2,
            grid=(m // tc,),
            in_specs=[pl.BlockSpec((tc, d), lambda i, p1, p2: (i, 0)),
                      pl.BlockSpec((tc, LANES), lambda i, p1, p2: (i, 0)),
                      pl.BlockSpec(memory_space=pl.ANY)],
            out_specs=pl.BlockSpec((tc, d), lambda i, p1, p2: (i, 0)),
            scratch_shapes=[pltpu.VMEM((TOP_K, tc, d), F32), pltpu.SemaphoreType.DMA(())]),
        out_shape=jax.ShapeDtypeStruct((m, d), F32),
        compiler_params=_params("arbitrary"),
        name="combine_expert_rows",
    )(p1, p2, x2, sel, y)


def _route(sel, tm):
    m = sel.shape[0]
    e_flat = sel[:, :TOP_K].astype(jnp.int32).T.reshape(-1)
    onehot = (e_flat[:, None] == jnp.arange(N_EXPERTS, dtype=jnp.int32)[None, :]).astype(jnp.int32)
    csum = jnp.cumsum(onehot, axis=0)
    rank = jnp.sum((csum - onehot) * onehot, axis=1)
    counts = csum[-1]
    padded = ((counts + tm - 1) // tm) * tm
    ends = jnp.cumsum(padded)
    dest = (ends - padded)[e_flat] + rank
    mp = TOP_K * m + N_EXPERTS * tm
    n_tiles = mp // tm
    tok = jnp.tile(jnp.arange(m, dtype=jnp.int32), TOP_K)
    src_tok = jnp.zeros((mp,), jnp.int32).at[dest].set(tok)
    n_used = (ends[-1] // tm).astype(jnp.int32)
    tile_id = jnp.arange(n_tiles, dtype=jnp.int32)
    tile_e = jnp.sum((ends[None, :] <= (tile_id * tm)[:, None]).astype(jnp.int32), axis=1)
    tile_e = jnp.minimum(tile_e, N_EXPERTS - 1)
    tile_e = jnp.where(tile_id < n_used, tile_e, tile_e[jnp.maximum(n_used - 1, 0)])
    return src_tok, tile_e, n_used.reshape(1), dest[:m], dest[m:]


def _moe(x2, g, router_w, wg, wu, wd):
    m, d = x2.shape
    tm = _tile(m, MOE_ROW_TILE)
    sel = _router(x2, g, router_w)
    src_tok, tile_e, n_used, p1, p2 = _route(sel, tm)
    y = _moe_ffn(tile_e, n_used, src_tok, x2, g, wg, wu, wd, tm=tm)
    return _combine(p1, p2, x2, sel, y)


def _pad_cols(a, n):
    return jnp.pad(a, ((0, 0), (0, n - a.shape[1])))


def _mixing_sublayer(x2, bsz, seq, norm_g, w_in, b_f, b_gate, conv_w, conv_b, cln_g, cln_b,
                     pool_w, pool_scale, q_g, k_g, w_br, w_o):
    row = lambda v: v.reshape(1, -1)
    w_main = w_in[:, :OFF_F].astype(BF16)
    w_f = _pad_cols(w_in[:, OFF_F:OFF_G], LANES).astype(BF16)
    w_gate = w_in[:, OFF_G:].astype(BF16)

    z, f_logit, h = _in_proj(x2, row(norm_g), w_main, w_f, _pad_cols(row(b_f), LANES))

    z3 = z.reshape(bsz, seq, OFF_F)
    c3 = _log_forget_cumsum(f_logit.reshape(bsz, seq, LANES))
    y_ab = _mixers(z3, conv_w, row(conv_b), row(cln_g), row(cln_b), pool_w.astype(BF16), row(pool_scale))
    qp, kp = _qk_prep(z3, c3, row(q_g), row(k_g))
    y_c = _attention(qp, kp, z3)

    merged = _merge(h, y_ab.reshape(bsz * seq, -1), y_c.reshape(bsz * seq, -1), w_br.astype(BF16),
                    w_gate, row(b_gate))
    return _out_proj(merged, w_o.astype(BF16), x2)


def kernel(x, norm1_g, w_in, b_f, b_gate, conv_w, conv_b, cln_g, cln_b, pool_w, pool_scale, q_g, k_g, w_br, w_o,
           norm2_g, ffn_wg, ffn_wu, ffn_wd, router, exp_wg, exp_wu, exp_wd):
    bsz, seq, d = x.shape
    depth = w_in.shape[0]
    x2 = x.reshape(bsz * seq, d)
    for l in range(depth):
        x2 = _mixing_sublayer(x2, bsz, seq, norm1_g[l], w_in[l], b_f[l], b_gate[l], conv_w[l], conv_b[l],
                              cln_g[l], cln_b[l], pool_w[l], pool_scale[l], q_g[l], k_g[l], w_br[l], w_o[l])
        g2 = norm2_g[l].reshape(1, d)
        j = l // 2
        if l % 2 == 0:
            x2 = _ffn(x2, g2, ffn_wg[j].astype(BF16), ffn_wu[j].astype(BF16), ffn_wd[j].astype(BF16))
        else:
            x2 = _moe(x2, g2, router[j], exp_wg[j].astype(BF16), exp_wu[j].astype(BF16), exp_wd[j].astype(BF16))
    return x2.reshape(bsz, seq, d)
```

```python
import functools

import jax
import jax.numpy as jnp
import numpy as np
from jax import lax
from jax.experimental import pallas as pl
from jax.experimental.pallas import tpu as pltpu

EPS = 1e-6
CONV_WIDTH = 512
CONV_K = 31
POOL_WIDTH = 512
POOL_WINDOWS = (2, 4, 8, 16)
POOL_GROUP = POOL_WIDTH // len(POOL_WINDOWS)
ATT_HEADS = 8
HEAD_DIM = 128
ATT_WIDTH = ATT_HEADS * HEAD_DIM
N_BRANCH = 3
N_EXPERTS = 8
TOP_K = 2

OFF_B = 2 * CONV_WIDTH
OFF_Q = OFF_B + POOL_WIDTH
OFF_K = OFF_Q + ATT_WIDTH
OFF_V = OFF_K + ATT_WIDTH
OFF_F = OFF_V + ATT_WIDTH
OFF_G = OFF_F + ATT_HEADS

LANES = 128
SUBLANES = 8
HALO = 32
NEG_BIG = -1e30
VMEM_LIMIT_BYTES = 56 * 1024 * 1024
MOE_ROW_TILE = 512

F32 = jnp.float32
BF16 = jnp.bfloat16


def _tile(n, pref):
    t = min(n, pref)
    while n % t:
        t -= 1
    return t


def _params(*sem):
    return pltpu.CompilerParams(dimension_semantics=sem, vmem_limit_bytes=VMEM_LIMIT_BYTES)


def _rmsnorm_rows(x, g):
    ms = jnp.mean(x * x, axis=-1, keepdims=True)
    return x * lax.rsqrt(ms + EPS) * g


def _sigmoid(x):
    return 1.0 / (1.0 + jnp.exp(-x))


def _log_sigmoid(x):
    return jnp.minimum(x, 0.0) - jnp.log1p(jnp.exp(-jnp.abs(x)))


def _top2(logits):
    lane = lax.broadcasted_iota(jnp.int32, logits.shape, 1).astype(F32)
    lg = jnp.where(lane < N_EXPERTS, logits, NEG_BIG)
    m1 = jnp.max(lg, axis=-1, keepdims=True)
    i1 = jnp.min(jnp.where(lg == m1, lane, float(LANES)), axis=-1, keepdims=True)
    lg2 = jnp.where(lane == i1, NEG_BIG, lg)
    m2 = jnp.max(lg2, axis=-1, keepdims=True)
    i2 = jnp.min(jnp.where(lg2 == m2, lane, float(LANES)), axis=-1, keepdims=True)
    e = jnp.exp(m2 - m1)
    w1 = 1.0 / (1.0 + e)
    w2 = e / (1.0 + e)
    return jnp.where(lane == 0, i1, jnp.where(lane == 1, i2, jnp.where(lane == 2, w1, jnp.where(lane == 3, w2, 0.0))))


def _split2(a):
    hi = a.astype(BF16)
    return hi, (a - hi.astype(F32)).astype(BF16)


def _router_kernel(x_ref, g_ref, whi_ref, wlo_ref, o_ref):
    h_hi, h_lo = _split2(_rmsnorm_rows(x_ref[...], g_ref[...]))
    logits = (jnp.dot(h_hi, whi_ref[...], preferred_element_type=F32)
              + jnp.dot(h_lo, whi_ref[...], preferred_element_type=F32)
              + jnp.dot(h_hi, wlo_ref[...], preferred_element_type=F32))
    o_ref[...] = _top2(logits)


def _router(x2, g, router_w, *, tm=1024):
    m, d = x2.shape
    tm = _tile(m, tm)
    w_hi, w_lo = _split2(_pad_cols(router_w, LANES))
    return pl.pallas_call(
        _router_kernel,
        grid=(m // tm,),
        in_specs=[pl.BlockSpec((tm, d), lambda i: (i, 0)),
                  pl.BlockSpec((1, d), lambda i: (0, 0)),
                  pl.BlockSpec((d, LANES), lambda i: (0, 0)),
                  pl.BlockSpec((d, LANES), lambda i: (0, 0))],
        out_specs=pl.BlockSpec((tm, LANES), lambda i: (i, 0)),
        out_shape=jax.ShapeDtypeStruct((m, LANES), F32),
        compiler_params=_params("parallel"),
        name="router_top2",
    )(x2, g, w_hi, w_lo)


def _in_proj_kernel(x_ref, g_ref, w_ref, wf_ref, bf_ref, o_ref, f_ref, h_ref):
    @pl.when(pl.program_id(1) == 0)
    def _():
        h = _rmsnorm_rows(x_ref[...], g_ref[...]).astype(BF16)
        h_ref[...] = h
        f_ref[...] = jnp.dot(h, wf_ref[...].astype(BF16), preferred_element_type=F32) + bf_ref[...]

    o_ref[...] = jnp.dot(h_ref[...], w_ref[...].astype(BF16), preferred_element_type=F32).astype(o_ref.dtype)


def _in_proj(x2, g, w_in, layer, b_f, *, tm=1024, tn=768):
    m, d = x2.shape
    n = OFF_F
    tm, tn = _tile(m, tm), _tile(n, tn)
    assert OFF_F % LANES == 0 and tn % LANES == 0
    return pl.pallas_call(
        _in_proj_kernel,
        grid=(m // tm, n // tn),
        in_specs=[pl.BlockSpec((tm, d), lambda i, j: (i, 0)),
                  pl.BlockSpec((1, d), lambda i, j: (0, 0)),
                  pl.BlockSpec((None, d, tn), lambda i, j: (layer, 0, j)),
                  pl.BlockSpec((None, d, LANES), lambda i, j: (layer, 0, OFF_F // LANES)),
                  pl.BlockSpec((1, LANES), lambda i, j: (0, 0))],
        out_specs=[pl.BlockSpec((tm, tn), lambda i, j: (i, j)),
                   pl.BlockSpec((tm, LANES), lambda i, j: (i, 0)),
                   pl.BlockSpec((tm, d), lambda i, j: (i, 0))],
        out_shape=[jax.ShapeDtypeStruct((m, n), BF16), jax.ShapeDtypeStruct((m, LANES), F32),
                   jax.ShapeDtypeStruct((m, d), BF16)],
        compiler_params=_params("parallel", "arbitrary"),
        name="in_proj",
    )(x2, g, w_in, w_in, b_f)


def _cumsum_kernel(f_ref, o_ref, *, chunk):
    s = f_ref.shape[0]
    row = lax.broadcasted_iota(jnp.int32, (chunk, chunk), 0)
    col = lax.broadcasted_iota(jnp.int32, (chunk, chunk), 1)
    tril = (row >= col).astype(F32)
    carry = jnp.zeros((1, f_ref.shape[1]), F32)
    for c in range(s // chunk):
        ls = _log_sigmoid(f_ref[c * chunk:(c + 1) * chunk, :])
        cs = jnp.dot(tril, ls, preferred_element_type=F32, precision=lax.Precision.HIGHEST) + carry
        o_ref[c * chunk:(c + 1) * chunk, :] = cs
        carry = cs[chunk - 1:chunk, :]


def _log_forget_cumsum(f3):
    b, s, n = f3.shape
    chunk = _tile(s, LANES)
    return pl.pallas_call(
        functools.partial(_cumsum_kernel, chunk=chunk),
        grid=(b,),
        in_specs=[pl.BlockSpec((None, s, n), lambda i: (i, 0, 0))],
        out_specs=pl.BlockSpec((None, s, n), lambda i: (i, 0, 0)),
        out_shape=jax.ShapeDtypeStruct((b, s, n), F32),
        compiler_params=_params("parallel"),
        name="log_forget_cumsum",
    )(f3)


def _shift_rows(win, p):
    if p == 0:
        return win
    return pltpu.roll(win, win.shape[0] - p, axis=0)


def _mixers_kernel(z_ref, cw_ref, cb_ref, lg_ref, lb_ref, pw_ref, ps_ref, o_ref, zp_ref, up_ref, cv_ref, *, rows):
    s = z_ref.shape[0]
    n_chunks = s // rows
    zp_ref[0:HALO, :] = jnp.zeros((HALO, CONV_WIDTH), F32)
    up_ref[0:HALO, :] = jnp.zeros((HALO, POOL_WIDTH), F32)

    def fill(c, _):
        r0 = pl.multiple_of(c * rows, rows)
        a = z_ref[pl.ds(r0, rows), 0:CONV_WIDTH].astype(F32)
        gt = z_ref[pl.ds(r0, rows), CONV_WIDTH:OFF_B].astype(F32)
        zp_ref[pl.ds(r0 + HALO, rows), :] = a * _sigmoid(gt)
        up_ref[pl.ds(r0 + HALO, rows), :] = z_ref[pl.ds(r0, rows), OFF_B:OFF_Q].astype(F32)
        return 0

    lax.fori_loop(0, n_chunks, fill, 0)

    lead = HALO - (CONV_K - 1)

    def mix(c, _):
        r0 = pl.multiple_of(c * rows, rows)
        for lg in range(CONV_WIDTH // LANES):
            ls = slice(lg * LANES, (lg + 1) * LANES)
            win = zp_ref[pl.ds(r0, rows + HALO), ls]
            acc = jnp.zeros((rows, LANES), F32) + cb_ref[:, ls]
            for p in range(SUBLANES):
                wp = _shift_rows(win, p)
                for k in range(CONV_K):
                    if (lead + k) % SUBLANES == p:
                        off = lead + k - p
                        acc = acc + cw_ref[k:k + 1, ls] * wp[off:off + rows, :]
            cv_ref[:, ls] = acc
        y = cv_ref[...]
        mu = jnp.mean(y, axis=-1, keepdims=True)
        yc = y - mu
        var = jnp.mean(yc * yc, axis=-1, keepdims=True)
        y = yc * lax.rsqrt(var + EPS) * lg_ref[...] + lb_ref[...]
        o_ref[pl.ds(r0, rows), 0:CONV_WIDTH] = (y * _sigmoid(y)).astype(o_ref.dtype)

        half = HALO // 2
        pos = (lax.broadcasted_iota(jnp.int32, (rows, LANES), 0) + (r0 + 1)).astype(F32)
        for gi, w in enumerate(POOL_WINDOWS):
            ls = slice(gi * POOL_GROUP, (gi + 1) * POOL_GROUP)
            win = up_ref[pl.ds(r0 + half, rows + half), ls]
            tot = win
            span = 1
            while span < w:
                tot = tot + pltpu.roll(tot, span, axis=0)
                span *= 2
            u = win[half:half + rows, :]
            p = tot[half:half + rows, :] / jnp.minimum(pos, float(w)) - u
            yb = jnp.dot(p.astype(BF16), pw_ref[gi], preferred_element_type=F32) * ps_ref[:, ls]
            o_ref[pl.ds(r0, rows), CONV_WIDTH + gi * POOL_GROUP:CONV_WIDTH + (gi + 1) * POOL_GROUP] = yb.astype(o_ref.dtype)
        return 0

    lax.fori_loop(0, n_chunks, mix, 0)


def _mixers(z3, conv_w, conv_b, cln_g, cln_b, pool_w, pool_scale):
    b, s, n = z3.shape
    rows = _tile(s, 256)
    width = OFF_Q
    assert n % width == 0
    return pl.pallas_call(
        functools.partial(_mixers_kernel, rows=rows),
        grid=(b,),
        in_specs=[pl.BlockSpec((None, s, width), lambda i: (i, 0, 0)),
                  pl.BlockSpec(conv_w.shape, lambda i: (0, 0)),
                  pl.BlockSpec(conv_b.shape, lambda i: (0, 0)),
                  pl.BlockSpec(cln_g.shape, lambda i: (0, 0)),
                  pl.BlockSpec(cln_b.shape, lambda i: (0, 0)),
                  pl.BlockSpec(pool_w.shape, lambda i: (0, 0, 0)),
                  pl.BlockSpec(pool_scale.shape, lambda i: (0, 0))],
        out_specs=pl.BlockSpec((None, s, CONV_WIDTH + POOL_WIDTH), lambda i: (i, 0, 0)),
        out_shape=jax.ShapeDtypeStruct((b, s, CONV_WIDTH + POOL_WIDTH), BF16),
        scratch_shapes=[pltpu.VMEM((s + HALO, CONV_WIDTH), F32),
                        pltpu.VMEM((s + HALO, POOL_WIDTH), F32),
                        pltpu.VMEM((rows, CONV_WIDTH), F32)],
        compiler_params=_params("parallel"),
        name="conv_pool_mixers",
    )(z3, conv_w, conv_b, cln_g, cln_b, pool_w, pool_scale)


def _split3(c):
    hi = c.astype(BF16).astype(F32)
    r1 = c - hi
    mid = r1.astype(BF16).astype(F32)
    return hi, mid, r1 - mid


N_PIECES = 3


def _placement():
    e = np.zeros((ATT_HEADS, N_PIECES * LANES, 2 * LANES), np.float32)
    for h in range(ATT_HEADS):
        for p in range(N_PIECES):
            e[h, p * LANES + h, p] = 1.0
            e[h, p * LANES + h, LANES + N_PIECES + p] = -1.0
    return jnp.asarray(e, BF16)


def _extension_ones():
    o = np.zeros((1, 2 * LANES), np.float32)
    o[0, N_PIECES:2 * N_PIECES] = 1.0
    o[0, LANES:LANES + N_PIECES] = 1.0
    return jnp.asarray(o)


def _mean_sq_lanes(x, ones):
    hi, lo = _split2(x * x)
    tot = jnp.dot(hi, ones, preferred_element_type=F32) + jnp.dot(lo, ones, preferred_element_type=F32)
    return tot * (1.0 / HEAD_DIM)


def _qk_prep_kernel(qa_ref, qb_ref, ka_ref, kb_ref, c_ref, qg_ref, kg_ref, place_ref, ext1_ref, ones_ref,
                    qo_ref, ko_ref):
    scale = HEAD_DIM ** -0.5
    pieces = jnp.concatenate(_split3(c_ref[...]), axis=1).astype(BF16)
    ones = ones_ref[...]
    q_gain = qg_ref[...] * scale
    k_gain = kg_ref[...]
    half = ATT_HEADS // 2
    for h in range(ATT_HEADS):
        q_src, k_src = (qa_ref, ka_ref) if h < half else (qb_ref, kb_ref)
        ls = slice((h % half) * HEAD_DIM, (h % half + 1) * HEAD_DIM)
        q = q_src[:, ls].astype(F32)
        k = k_src[:, ls].astype(F32)
        qn = q * lax.rsqrt(_mean_sq_lanes(q, ones) + EPS) * q_gain
        kn = k * lax.rsqrt(_mean_sq_lanes(k, ones) + EPS) * k_gain
        ext = jnp.dot(pieces, place_ref[h], preferred_element_type=F32) + ext1_ref[...]
        qo_ref[h, :, 0:HEAD_DIM] = qn.astype(BF16)
        qo_ref[h, :, HEAD_DIM:2 * HEAD_DIM] = ext[:, 0:LANES].astype(BF16)
        ko_ref[h, :, 0:HEAD_DIM] = kn.astype(BF16)
        ko_ref[h, :, HEAD_DIM:2 * HEAD_DIM] = ext[:, LANES:2 * LANES].astype(BF16)


def _qk_prep(z3, c3, q_g, k_g):
    b, s, _ = z3.shape
    ts = _tile(s, 512)
    blk = (ATT_HEADS // 2) * HEAD_DIM
    qb0, kb0 = OFF_Q // blk, OFF_K // blk
    assert OFF_Q % blk == 0 and OFF_K % blk == 0
    out = jax.ShapeDtypeStruct((b, ATT_HEADS, s, 2 * HEAD_DIM), BF16)
    zspec = lambda cb: pl.BlockSpec((None, ts, blk), lambda i, j, cb=cb: (i, j, cb))
    ospec = pl.BlockSpec((None, ATT_HEADS, ts, 2 * HEAD_DIM), lambda i, j: (i, 0, j, 0))
    return pl.pallas_call(
        _qk_prep_kernel,
        grid=(b, s // ts),
        in_specs=[zspec(qb0), zspec(qb0 + 1), zspec(kb0), zspec(kb0 + 1),
                  pl.BlockSpec((None, ts, LANES), lambda i, j: (i, j, 0)),
                  pl.BlockSpec((1, HEAD_DIM), lambda i, j: (0, 0)),
                  pl.BlockSpec((1, HEAD_DIM), lambda i, j: (0, 0)),
                  pl.BlockSpec((ATT_HEADS, N_PIECES * LANES, 2 * LANES), lambda i, j: (0, 0, 0)),
                  pl.BlockSpec((1, 2 * LANES), lambda i, j: (0, 0)),
                  pl.BlockSpec((HEAD_DIM, HEAD_DIM), lambda i, j: (0, 0))],
        out_specs=[ospec, ospec],
        out_shape=[out, out],
        compiler_params=_params("parallel", "parallel"),
        name="qk_prep",
    )(z3, z3, z3, z3, c3, q_g, k_g, _placement(), _extension_ones(), jnp.ones((HEAD_DIM, HEAD_DIM), BF16))


ATT_HEADS_PER_STEP = 2


def _attn_kernel(q_ref, k_ref, v_ref, o_ref, *, tq):
    heads, s_len, _ = q_ref.shape
    for hh in range(heads):
        ls = slice(hh * HEAD_DIM, (hh + 1) * HEAD_DIM)
        for i in range(s_len // tq):
            n = (i + 1) * tq
            q = q_ref[hh, i * tq:(i + 1) * tq, :]
            s = lax.dot_general(q, k_ref[hh, 0:n, :], (((1,), (1,)), ((), ())), preferred_element_type=F32)
            row = lax.broadcasted_iota(jnp.int32, s.shape, 0) + i * tq
            col = lax.broadcasted_iota(jnp.int32, s.shape, 1)
            s = jnp.where(col <= row, s, NEG_BIG)
            m = jnp.max(s, axis=-1, keepdims=True)
            p = jnp.exp(s - m)
            l = jnp.sum(p, axis=-1, keepdims=True)
            acc = jnp.dot(p.astype(BF16), v_ref[0:n, ls], preferred_element_type=F32)
            o_ref[i * tq:(i + 1) * tq, ls] = (acc / l).astype(o_ref.dtype)


def _attention(qp, kp, z3):
    b, h, s, dk = qp.shape
    tq = _tile(s, 512)
    hs = ATT_HEADS_PER_STEP
    width = hs * HEAD_DIM
    assert h % hs == 0 and OFF_V % width == 0
    v0 = OFF_V // width
    return pl.pallas_call(
        functools.partial(_attn_kernel, tq=tq),
        grid=(b, h // hs),
        in_specs=[pl.BlockSpec((None, hs, s, dk), lambda bi, hi: (bi, hi, 0, 0)),
                  pl.BlockSpec((None, hs, s, dk), lambda bi, hi: (bi, hi, 0, 0)),
                  pl.BlockSpec((None, s, width), lambda bi, hi: (bi, 0, v0 + hi))],
        out_specs=pl.BlockSpec((None, s, width), lambda bi, hi: (bi, 0, hi)),
        out_shape=jax.ShapeDtypeStruct((b, s, ATT_WIDTH), BF16),
        compiler_params=_params("parallel", "parallel"),
        name="fox_attention",
    )(qp, kp, z3)


def _merge_kernel(h_ref, ya_ref, yb_ref, yc_ref, wa_ref, wb_ref, wc_ref,
                  wg0_ref, wg1_ref, wg2_ref, bg0_ref, bg1_ref, bg2_ref, o_ref):
    h = h_ref[...]

    def gated(y_ref, w_ref, wg_ref, bg_ref):
        gate = _sigmoid(jnp.dot(h, wg_ref[...], preferred_element_type=F32) + bg_ref[...])
        return gate * jnp.dot(y_ref[...], w_ref[...], preferred_element_type=F32)

    merged = (gated(ya_ref, wa_ref, wg0_ref, bg0_ref) + gated(yb_ref, wb_ref, wg1_ref, bg1_ref)
              + gated(yc_ref, wc_ref, wg2_ref, bg2_ref))
    o_ref[...] = merged.astype(o_ref.dtype)


def _merge(h, y_ab, y_c, w_br, w_gate, b_gate, *, tm=1024, tn=512):
    m, d = h.shape
    tm, tn = _tile(m, tm), _tile(d, tn)
    nj = d // tn
    gate_w = lambda br: pl.BlockSpec((d, tn), lambda i, j, br=br: (0, br * nj + j))
    gate_b = lambda br: pl.BlockSpec((1, tn), lambda i, j, br=br: (0, br * nj + j))
    return pl.pallas_call(
        _merge_kernel,
        grid=(m // tm, nj),
        in_specs=[pl.BlockSpec((tm, d), lambda i, j: (i, 0)),
                  pl.BlockSpec((tm, CONV_WIDTH), lambda i, j: (i, 0)),
                  pl.BlockSpec((tm, POOL_WIDTH), lambda i, j: (i, 1)),
                  pl.BlockSpec((tm, ATT_WIDTH), lambda i, j: (i, 0)),
                  pl.BlockSpec((CONV_WIDTH, tn), lambda i, j: (0, j)),
                  pl.BlockSpec((POOL_WIDTH, tn), lambda i, j: (1, j)),
                  pl.BlockSpec((ATT_WIDTH, tn), lambda i, j: (1, j)),
                  gate_w(0), gate_w(1), gate_w(2), gate_b(0), gate_b(1), gate_b(2)],
        out_specs=pl.BlockSpec((tm, tn), lambda i, j: (i, j)),
        out_shape=jax.ShapeDtypeStruct((m, d), BF16),
        compiler_params=_params("parallel", "arbitrary"),
        name="gated_branch_merge",
    )(h, y_ab, y_ab, y_c, w_br, w_br, w_br, w_gate, w_gate, w_gate, b_gate, b_gate, b_gate)


def _out_proj_kernel(a_ref, w_ref, x_ref, o_ref):
    o_ref[...] = x_ref[...] + jnp.dot(a_ref[...], w_ref[...], preferred_element_type=F32)


def _out_proj(a, w, x2, *, tm=1024, tn=1024):
    m, k = a.shape
    n = w.shape[1]
    tm, tn = _tile(m, tm), _tile(n, tn)
    return pl.pallas_call(
        _out_proj_kernel,
        grid=(m // tm, n // tn),
        in_specs=[pl.BlockSpec((tm, k), lambda i, j: (i, 0)),
                  pl.BlockSpec((k, tn), lambda i, j: (0, j)),
                  pl.BlockSpec((tm, tn), lambda i, j: (i, j))],
        out_specs=pl.BlockSpec((tm, tn), lambda i, j: (i, j)),
        out_shape=jax.ShapeDtypeStruct((m, n), F32),
        compiler_params=_params("parallel", "arbitrary"),
        name="out_proj_residual",
    )(a, w, x2)


def _swiglu_partial(h, wg, wu, wd):
    a = jnp.dot(h, wg, preferred_element_type=F32)
    u = jnp.dot(h, wu, preferred_element_type=F32)
    act = (a * _sigmoid(a) * u).astype(BF16)
    return jnp.dot(act, wd, preferred_element_type=F32)


def _ffn_kernel(x_ref, g_ref, wg_ref, wu_ref, wd_ref, o_ref, h_ref, acc_ref):
    f = pl.program_id(1)

    @pl.when(f == 0)
    def _():
        h_ref[...] = _rmsnorm_rows(x_ref[...], g_ref[...]).astype(BF16)
        acc_ref[...] = jnp.zeros(acc_ref.shape, F32)

    acc_ref[...] += _swiglu_partial(h_ref[...], wg_ref[...], wu_ref[...], wd_ref[...])

    @pl.when(f == pl.num_programs(1) - 1)
    def _():
        o_ref[...] = x_ref[...] + acc_ref[...]


def _ffn(x2, g, wg, wu, wd, *, tm=512, tf=512):
    m, d = x2.shape
    ff = wg.shape[1]
    tm, tf = _tile(m, tm), _tile(ff, tf)
    return pl.pallas_call(
        _ffn_kernel,
        grid=(m // tm, ff // tf),
        in_specs=[pl.BlockSpec((tm, d), lambda i, f: (i, 0)),
                  pl.BlockSpec((1, d), lambda i, f: (0, 0)),
                  pl.BlockSpec((d, tf), lambda i, f: (0, f)),
                  pl.BlockSpec((d, tf), lambda i, f: (0, f)),
                  pl.BlockSpec((tf, d), lambda i, f: (f, 0))],
        out_specs=pl.BlockSpec((tm, d), lambda i, f: (i, 0)),
        out_shape=jax.ShapeDtypeStruct((m, d), F32),
        scratch_shapes=[pltpu.VMEM((tm, d), BF16), pltpu.VMEM((tm, d), F32)],
        compiler_params=_params("parallel", "arbitrary"),
        name="swiglu_dense",
    )(x2, g, wg, wu, wd)


def _moe_ffn_kernel(te_ref, nu_ref, tok_ref, x_ref, g_ref, wg_ref, wu_ref, wd_ref, o_ref,
                    xbuf_ref, h_ref, acc_ref, sem, *, tm, rows_step):
    i = pl.program_id(0)
    f = pl.program_id(1)
    last = pl.num_programs(1) - 1
    n_used = nu_ref[0]
    used = i < n_used
    slot = i % 2
    rows_tile = xbuf_ref.shape[1]

    def row_copy(tile, r, slot_):
        return pltpu.make_async_copy(x_ref.at[pl.ds(tok_ref[tile * tm + r], 1)],
                                     xbuf_ref.at[slot_, pl.ds(r, 1)], sem.at[slot_])

    @pl.when((i == 0) & (f == 0))
    def _():
        def body(r, c):
            row_copy(0, r, 0).start()
            return c
        lax.fori_loop(0, rows_tile, body, 0, unroll=8)

    @pl.when((f == 0) & (i <= n_used))
    def _():
        def body(r, c):
            row_copy(i, r, slot).wait()
            return c
        lax.fori_loop(0, rows_tile, body, 0, unroll=8)

    @pl.when(used & (f == 0))
    def _():
        h_ref[...] = _rmsnorm_rows(xbuf_ref[slot, pl.ds(0, tm), :], g_ref[...]).astype(BF16)
        acc_ref[...] = jnp.zeros(acc_ref.shape, F32)

    @pl.when(used)
    def _():
        for r in range(rows_step):
            row_copy(i + 1, f * rows_step + r, 1 - slot).start()
        acc_ref[...] += _swiglu_partial(h_ref[...], wg_ref[...], wu_ref[...], wd_ref[...])

    @pl.when(f == last)
    def _():
        o_ref[...] = jnp.where(used, acc_ref[...], 0.0)


def _moe_ffn(tile_expert, n_used, src_tok, x2, g, wg, wu, wd, *, tm, tf=512):
    mp = src_tok.shape[0]
    d = x2.shape[1]
    ff = wg.shape[2]
    tf = _tile(ff, tf)
    nf = ff // tf
    rows_step = -(-tm // (nf * SUBLANES)) * SUBLANES
    rows_tile = rows_step * nf
    src_tok = jnp.pad(src_tok, (0, 2 * rows_tile))

    def wspec(shape, fdim):
        def imap(i, f, te, nu, tok):
            fi = jnp.where(i < nu[0], f, nf - 1)
            return (te[i], fi, 0) if fdim == 1 else (te[i], 0, fi)
        return pl.BlockSpec(shape, imap)

    return pl.pallas_call(
        functools.partial(_moe_ffn_kernel, tm=tm, rows_step=rows_step),
        grid_spec=pltpu.PrefetchScalarGridSpec(
            num_scalar_prefetch=3,
            grid=(mp // tm, nf),
            in_specs=[pl.BlockSpec(memory_space=pl.ANY),
                      pl.BlockSpec((1, d), lambda i, f, te, nu, tok: (0, 0)),
                      wspec((None, d, tf), 2),
                      wspec((None, d, tf), 2),
                      wspec((None, tf, d), 1)],
            out_specs=pl.BlockSpec((tm, d), lambda i, f, te, nu, tok: (i, 0)),
            scratch_shapes=[pltpu.VMEM((2, rows_tile, d), F32), pltpu.VMEM((tm, d), BF16),
                            pltpu.VMEM((tm, d), F32), pltpu.SemaphoreType.DMA((2,))]),
        out_shape=jax.ShapeDtypeStruct((mp, d), F32),
        compiler_params=_params("arbitrary", "arbitrary"),
        name="swiglu_experts_grouped",
    )(tile_expert, n_used, src_tok, x2, g, wg, wu, wd)


def _combine_kernel(p1_ref, p2_ref, x_ref, sel_ref, y_ref, o_ref, buf_ref, sem, *, tc):
    base = pl.program_id(0) * tc

    def row_copy(src_row, slot, r):
        return pltpu.make_async_copy(y_ref.at[pl.ds(src_row, 1)], buf_ref.at[slot, pl.ds(r, 1)], sem)

    def issue(grp, c):
        r0 = pl.multiple_of(grp * SUBLANES, SUBLANES)
        for k in range(SUBLANES):
            row_copy(p1_ref[base + r0 + k], 0, r0 + k).start()
            row_copy(p2_ref[base + r0 + k], 1, r0 + k).start()
        return c

    def drain(r, c):
        row_copy(0, 0, r).wait()
        row_copy(0, 1, r).wait()
        return c

    lax.fori_loop(0, tc // SUBLANES, issue, 0)
    lax.fori_loop(0, tc, drain, 0, unroll=8)
    sel = sel_ref[...]
    w1, w2 = sel[:, TOP_K:TOP_K + 1], sel[:, TOP_K + 1:TOP_K + 2]
    o_ref[...] = x_ref[...] + w1 * buf_ref[0] + w2 * buf_ref[1]


def _combine(p1, p2, x2, sel, y, *, tc=512):
    m, d = x2.shape
    tc = _tile(m, tc)
    return pl.pallas_call(
        functools.partial(_combine_kernel, tc=tc),
        grid_spec=pltpu.PrefetchScalarGridSpec(
            num_scalar_prefetch=2,
            grid=(m // tc,),
            in_specs=[pl.BlockSpec((tc, d), lambda i, p1, p2: (i, 0)),
                      pl.BlockSpec((tc, LANES), lambda i, p1, p2: (i, 0)),
                      pl.BlockSpec(memory_space=pl.ANY)],
            out_specs=pl.BlockSpec((tc, d), lambda i, p1, p2: (i, 0)),
            scratch_shapes=[pltpu.VMEM((TOP_K, tc, d), F32), pltpu.SemaphoreType.DMA(())]),
        out_shape=jax.ShapeDtypeStruct((m, d), F32),
        compiler_params=_params("arbitrary"),
        name="combine_expert_rows",
    )(p1, p2, x2, sel, y)


def _route(sel, tm):
    m = sel.shape[0]
    e_flat = sel[:, :TOP_K].astype(jnp.int32).T.reshape(-1)
    onehot = (e_flat[:, None] == jnp.arange(N_EXPERTS, dtype=jnp.int32)[None, :]).astype(jnp.int32)
    csum = jnp.cumsum(onehot, axis=0)
    rank = jnp.sum((csum - onehot) * onehot, axis=1)
    counts = csum[-1]
    padded = ((counts + tm - 1) // tm) * tm
    ends = jnp.cumsum(padded)
    dest = (ends - padded)[e_flat] + rank
    mp = TOP_K * m + N_EXPERTS * tm
    n_tiles = mp // tm
    tok = jnp.tile(jnp.arange(m, dtype=jnp.int32), TOP_K)
    src_tok = jnp.zeros((mp,), jnp.int32).at[dest].set(tok)
    n_used = (ends[-1] // tm).astype(jnp.int32)
    tile_id = jnp.arange(n_tiles, dtype=jnp.int32)
    tile_e = jnp.sum((ends[None, :] <= (tile_id * tm)[:, None]).astype(jnp.int32), axis=1)
    tile_e = jnp.minimum(tile_e, N_EXPERTS - 1)
    tile_e = jnp.where(tile_id < n_used, tile_e, tile_e[jnp.maximum(n_used - 1, 0)])
    return src_tok, tile_e, n_used.reshape(1), dest[:m], dest[m:]


def _moe(x2, g, router_w, wg, wu, wd):
    m, d = x2.shape
    tm = _tile(m, MOE_ROW_TILE)
    sel = _router(x2, g, router_w)
    src_tok, tile_e, n_used, p1, p2 = _route(sel, tm)
    y = _moe_ffn(tile_e, n_used, src_tok, x2, g, wg, wu, wd, tm=tm)
    return _combine(p1, p2, x2, sel, y)


def _pad_cols(a, n):
    return jnp.pad(a, ((0, 0), (0, n - a.shape[1])))


def _mixing_sublayer(x2, bsz, seq, layer, norm_g, w_in, b_f, b_gate, conv_w, conv_b, cln_g, cln_b,
                     pool_w, pool_scale, q_g, k_g, w_br, w_o):
    row = lambda v: v.reshape(1, -1)
    w_gate = w_in[layer, :, OFF_G:].astype(BF16)

    z, f_logit, h = _in_proj(x2, row(norm_g), w_in, layer, _pad_cols(row(b_f), LANES))

    z3 = z.reshape(bsz, seq, OFF_F)
    c3 = _log_forget_cumsum(f_logit.reshape(bsz, seq, LANES))
    y_ab = _mixers(z3, conv_w, row(conv_b), row(cln_g), row(cln_b), pool_w.astype(BF16), row(pool_scale))
    qp, kp = _qk_prep(z3, c3, row(q_g), row(k_g))
    y_c = _attention(qp, kp, z3)

    merged = _merge(h, y_ab.reshape(bsz * seq, -1), y_c.reshape(bsz * seq, -1), w_br.astype(BF16),
                    w_gate, row(b_gate))
    return _out_proj(merged, w_o.astype(BF16), x2)


def kernel(x, norm1_g, w_in, b_f, b_gate, conv_w, conv_b, cln_g, cln_b, pool_w, pool_scale, q_g, k_g, w_br, w_o,
           norm2_g, ffn_wg, ffn_wu, ffn_wd, router, exp_wg, exp_wu, exp_wd):
    bsz, seq, d = x.shape
    depth = w_in.shape[0]
    x2 = x.reshape(bsz * seq, d)
    for l in range(depth):
        x2 = _mixing_sublayer(x2, bsz, seq, l, norm1_g[l], w_in, b_f[l], b_gate[l], conv_w[l], conv_b[l],
                              cln_g[l], cln_b[l], pool_w[l], pool_scale[l], q_g[l], k_g[l], w_br[l], w_o[l])
        g2 = norm2_g[l].reshape(1, d)
        j = l // 2
        if l % 2 == 0:
            x2 = _ffn(x2, g2, ffn_wg[j].astype(BF16), ffn_wu[j].astype(BF16), ffn_wd[j].astype(BF16))
        else:
            x2 = _moe(x2, g2, router[j], exp_wg[j].astype(BF16), exp_wu[j].astype(BF16), exp_wd[j].astype(BF16))
    return x2.reshape(bsz, seq, d)
```

```python
import functools

import jax
import jax.numpy as jnp
import numpy as np
from jax import lax
from jax.experimental import pallas as pl
from jax.experimental.pallas import tpu as pltpu

EPS = 1e-6
CONV_WIDTH = 512
CONV_K = 31
POOL_WIDTH = 512
POOL_WINDOWS = (2, 4, 8, 16)
POOL_GROUP = POOL_WIDTH // len(POOL_WINDOWS)
ATT_HEADS = 8
HEAD_DIM = 128
ATT_WIDTH = ATT_HEADS * HEAD_DIM
N_BRANCH = 3
N_EXPERTS = 8
TOP_K = 2

OFF_B = 2 * CONV_WIDTH
OFF_Q = OFF_B + POOL_WIDTH
OFF_K = OFF_Q + ATT_WIDTH
OFF_V = OFF_K + ATT_WIDTH
OFF_F = OFF_V + ATT_WIDTH
OFF_G = OFF_F + ATT_HEADS

LANES = 128
SUBLANES = 8
HALO = 32
NEG_BIG = -1e30
VMEM_LIMIT_BYTES = 56 * 1024 * 1024
MOE_ROW_TILE = 512

F32 = jnp.float32
BF16 = jnp.bfloat16


def _tile(n, pref):
    t = min(n, pref)
    while n % t:
        t -= 1
    return t


def _params(*sem):
    return pltpu.CompilerParams(dimension_semantics=sem, vmem_limit_bytes=VMEM_LIMIT_BYTES)


def _rmsnorm_rows(x, g):
    ms = jnp.mean(x * x, axis=-1, keepdims=True)
    return x * lax.rsqrt(ms + EPS) * g


def _sigmoid(x):
    return 1.0 / (1.0 + jnp.exp(-x))


def _log_sigmoid(x):
    return jnp.minimum(x, 0.0) - jnp.log1p(jnp.exp(-jnp.abs(x)))


def _top2(logits):
    lane = lax.broadcasted_iota(jnp.int32, logits.shape, 1).astype(F32)
    lg = jnp.where(lane < N_EXPERTS, logits, NEG_BIG)
    m1 = jnp.max(lg, axis=-1, keepdims=True)
    i1 = jnp.min(jnp.where(lg == m1, lane, float(LANES)), axis=-1, keepdims=True)
    lg2 = jnp.where(lane == i1, NEG_BIG, lg)
    m2 = jnp.max(lg2, axis=-1, keepdims=True)
    i2 = jnp.min(jnp.where(lg2 == m2, lane, float(LANES)), axis=-1, keepdims=True)
    e = jnp.exp(m2 - m1)
    w1 = 1.0 / (1.0 + e)
    w2 = e / (1.0 + e)
    return jnp.where(lane == 0, i1, jnp.where(lane == 1, i2, jnp.where(lane == 2, w1, jnp.where(lane == 3, w2, 0.0))))


def _split2(a):
    hi = a.astype(BF16)
    return hi, (a - hi.astype(F32)).astype(BF16)


def _router_kernel(x_ref, g_ref, whi_ref, wlo_ref, o_ref):
    h_hi, h_lo = _split2(_rmsnorm_rows(x_ref[...], g_ref[...]))
    logits = (jnp.dot(h_hi, whi_ref[...], preferred_element_type=F32)
              + jnp.dot(h_lo, whi_ref[...], preferred_element_type=F32)
              + jnp.dot(h_hi, wlo_ref[...], preferred_element_type=F32))
    o_ref[...] = _top2(logits)


def _router(x2, g, router_w, *, tm=1024):
    m, d = x2.shape
    tm = _tile(m, tm)
    w_hi, w_lo = _split2(_pad_cols(router_w, LANES))
    return pl.pallas_call(
        _router_kernel,
        grid=(m // tm,),
        in_specs=[pl.BlockSpec((tm, d), lambda i: (i, 0)),
                  pl.BlockSpec((1, d), lambda i: (0, 0)),
                  pl.BlockSpec((d, LANES), lambda i: (0, 0)),
                  pl.BlockSpec((d, LANES), lambda i: (0, 0))],
        out_specs=pl.BlockSpec((tm, LANES), lambda i: (i, 0)),
        out_shape=jax.ShapeDtypeStruct((m, LANES), F32),
        compiler_params=_params("parallel"),
        name="router_top2",
    )(x2, g, w_hi, w_lo)


def _in_proj_kernel(x_ref, g_ref, w_ref, wf_ref, bf_ref, o_ref, f_ref, h_ref):
    @pl.when(pl.program_id(1) == 0)
    def _():
        h = _rmsnorm_rows(x_ref[...], g_ref[...]).astype(BF16)
        h_ref[...] = h
        f_ref[...] = jnp.dot(h, wf_ref[...], preferred_element_type=F32) + bf_ref[...]

    o_ref[...] = jnp.dot(h_ref[...], w_ref[...], preferred_element_type=F32).astype(o_ref.dtype)


def _in_proj(x2, g, w, w_f, layer, b_f, *, tm=1024, tn=1536):
    m, d = x2.shape
    n = w.shape[2]
    tm, tn = _tile(m, tm), _tile(n, tn)
    return pl.pallas_call(
        _in_proj_kernel,
        grid=(m // tm, n // tn),
        in_specs=[pl.BlockSpec((tm, d), lambda i, j: (i, 0)),
                  pl.BlockSpec((1, d), lambda i, j: (0, 0)),
                  pl.BlockSpec((None, d, tn), lambda i, j: (layer, 0, j)),
                  pl.BlockSpec((None, d, LANES), lambda i, j: (layer, 0, 0)),
                  pl.BlockSpec((1, LANES), lambda i, j: (0, 0))],
        out_specs=[pl.BlockSpec((tm, tn), lambda i, j: (i, j)),
                   pl.BlockSpec((tm, LANES), lambda i, j: (i, 0)),
                   pl.BlockSpec((tm, d), lambda i, j: (i, 0))],
        out_shape=[jax.ShapeDtypeStruct((m, n), BF16), jax.ShapeDtypeStruct((m, LANES), F32),
                   jax.ShapeDtypeStruct((m, d), BF16)],
        compiler_params=_params("parallel", "arbitrary"),
        name="in_proj",
    )(x2, g, w, w_f, b_f)


def _cumsum_kernel(f_ref, o_ref, *, chunk):
    s = f_ref.shape[0]
    row = lax.broadcasted_iota(jnp.int32, (chunk, chunk), 0)
    col = lax.broadcasted_iota(jnp.int32, (chunk, chunk), 1)
    tril = (row >= col).astype(F32)
    carry = jnp.zeros((1, f_ref.shape[1]), F32)
    for c in range(s // chunk):
        ls = _log_sigmoid(f_ref[c * chunk:(c + 1) * chunk, :])
        cs = jnp.dot(tril, ls, preferred_element_type=F32, precision=lax.Precision.HIGHEST) + carry
        o_ref[c * chunk:(c + 1) * chunk, :] = cs
        carry = cs[chunk - 1:chunk, :]


def _log_forget_cumsum(f3):
    b, s, n = f3.shape
    chunk = _tile(s, LANES)
    return pl.pallas_call(
        functools.partial(_cumsum_kernel, chunk=chunk),
        grid=(b,),
        in_specs=[pl.BlockSpec((None, s, n), lambda i: (i, 0, 0))],
        out_specs=pl.BlockSpec((None, s, n), lambda i: (i, 0, 0)),
        out_shape=jax.ShapeDtypeStruct((b, s, n), F32),
        compiler_params=_params("parallel"),
        name="log_forget_cumsum",
    )(f3)


def _shift_rows(win, p):
    if p == 0:
        return win
    return pltpu.roll(win, win.shape[0] - p, axis=0)


def _mixers_kernel(z_ref, cw_ref, cb_ref, lg_ref, lb_ref, pw_ref, ps_ref, o_ref, zp_ref, up_ref, cv_ref, *, rows):
    s = z_ref.shape[0]
    n_chunks = s // rows
    zp_ref[0:HALO, :] = jnp.zeros((HALO, CONV_WIDTH), F32)
    up_ref[0:HALO, :] = jnp.zeros((HALO, POOL_WIDTH), F32)

    def fill(c, _):
        r0 = pl.multiple_of(c * rows, rows)
        a = z_ref[pl.ds(r0, rows), 0:CONV_WIDTH].astype(F32)
        gt = z_ref[pl.ds(r0, rows), CONV_WIDTH:OFF_B].astype(F32)
        zp_ref[pl.ds(r0 + HALO, rows), :] = a * _sigmoid(gt)
        up_ref[pl.ds(r0 + HALO, rows), :] = z_ref[pl.ds(r0, rows), OFF_B:OFF_Q].astype(F32)
        return 0

    lax.fori_loop(0, n_chunks, fill, 0)

    lead = HALO - (CONV_K - 1)

    def mix(c, _):
        r0 = pl.multiple_of(c * rows, rows)
        for lg in range(CONV_WIDTH // LANES):
            ls = slice(lg * LANES, (lg + 1) * LANES)
            win = zp_ref[pl.ds(r0, rows + HALO), ls]
            acc = jnp.zeros((rows, LANES), F32) + cb_ref[:, ls]
            for p in range(SUBLANES):
                wp = _shift_rows(win, p)
                for k in range(CONV_K):
                    if (lead + k) % SUBLANES == p:
                        off = lead + k - p
                        acc = acc + cw_ref[k:k + 1, ls] * wp[off:off + rows, :]
            cv_ref[:, ls] = acc
        y = cv_ref[...]
        mu = jnp.mean(y, axis=-1, keepdims=True)
        yc = y - mu
        var = jnp.mean(yc * yc, axis=-1, keepdims=True)
        y = yc * lax.rsqrt(var + EPS) * lg_ref[...] + lb_ref[...]
        o_ref[pl.ds(r0, rows), 0:CONV_WIDTH] = (y * _sigmoid(y)).astype(o_ref.dtype)

        half = HALO // 2
        pos = (lax.broadcasted_iota(jnp.int32, (rows, LANES), 0) + (r0 + 1)).astype(F32)
        for gi, w in enumerate(POOL_WINDOWS):
            ls = slice(gi * POOL_GROUP, (gi + 1) * POOL_GROUP)
            win = up_ref[pl.ds(r0 + half, rows + half), ls]
            tot = win
            span = 1
            while span < w:
                tot = tot + pltpu.roll(tot, span, axis=0)
                span *= 2
            u = win[half:half + rows, :]
            p = tot[half:half + rows, :] / jnp.minimum(pos, float(w)) - u
            yb = jnp.dot(p.astype(BF16), pw_ref[gi], preferred_element_type=F32) * ps_ref[:, ls]
            o_ref[pl.ds(r0, rows), CONV_WIDTH + gi * POOL_GROUP:CONV_WIDTH + (gi + 1) * POOL_GROUP] = yb.astype(o_ref.dtype)
        return 0

    lax.fori_loop(0, n_chunks, mix, 0)


def _mixers(z3, conv_w, conv_b, cln_g, cln_b, pool_w, pool_scale):
    b, s, n = z3.shape
    rows = _tile(s, 256)
    width = OFF_Q
    assert n % width == 0
    return pl.pallas_call(
        functools.partial(_mixers_kernel, rows=rows),
        grid=(b,),
        in_specs=[pl.BlockSpec((None, s, width), lambda i: (i, 0, 0)),
                  pl.BlockSpec(conv_w.shape, lambda i: (0, 0)),
                  pl.BlockSpec(conv_b.shape, lambda i: (0, 0)),
                  pl.BlockSpec(cln_g.shape, lambda i: (0, 0)),
                  pl.BlockSpec(cln_b.shape, lambda i: (0, 0)),
                  pl.BlockSpec(pool_w.shape, lambda i: (0, 0, 0)),
                  pl.BlockSpec(pool_scale.shape, lambda i: (0, 0))],
        out_specs=pl.BlockSpec((None, s, CONV_WIDTH + POOL_WIDTH), lambda i: (i, 0, 0)),
        out_shape=jax.ShapeDtypeStruct((b, s, CONV_WIDTH + POOL_WIDTH), BF16),
        scratch_shapes=[pltpu.VMEM((s + HALO, CONV_WIDTH), F32),
                        pltpu.VMEM((s + HALO, POOL_WIDTH), F32),
                        pltpu.VMEM((rows, CONV_WIDTH), F32)],
        compiler_params=_params("parallel"),
        name="conv_pool_mixers",
    )(z3, conv_w, conv_b, cln_g, cln_b, pool_w, pool_scale)


def _split3(c):
    hi = c.astype(BF16).astype(F32)
    r1 = c - hi
    mid = r1.astype(BF16).astype(F32)
    return hi, mid, r1 - mid


N_PIECES = 3


def _placement():
    e = np.zeros((ATT_HEADS, N_PIECES * LANES, 2 * LANES), np.float32)
    for h in range(ATT_HEADS):
        for p in range(N_PIECES):
            e[h, p * LANES + h, p] = 1.0
            e[h, p * LANES + h, LANES + N_PIECES + p] = -1.0
    return jnp.asarray(e, BF16)


def _extension_ones():
    o = np.zeros((1, 2 * LANES), np.float32)
    o[0, N_PIECES:2 * N_PIECES] = 1.0
    o[0, LANES:LANES + N_PIECES] = 1.0
    return jnp.asarray(o)


def _mean_sq_lanes(x, ones):
    hi, lo = _split2(x * x)
    tot = jnp.dot(hi, ones, preferred_element_type=F32) + jnp.dot(lo, ones, preferred_element_type=F32)
    return tot * (1.0 / HEAD_DIM)


def _qk_prep_kernel(qa_ref, qb_ref, ka_ref, kb_ref, c_ref, qg_ref, kg_ref, place_ref, ext1_ref, ones_ref,
                    qo_ref, ko_ref):
    scale = HEAD_DIM ** -0.5
    pieces = jnp.concatenate(_split3(c_ref[...]), axis=1).astype(BF16)
    ones = ones_ref[...]
    q_gain = qg_ref[...] * scale
    k_gain = kg_ref[...]
    half = ATT_HEADS // 2
    for h in range(ATT_HEADS):
        q_src, k_src = (qa_ref, ka_ref) if h < half else (qb_ref, kb_ref)
        ls = slice((h % half) * HEAD_DIM, (h % half + 1) * HEAD_DIM)
        q = q_src[:, ls].astype(F32)
        k = k_src[:, ls].astype(F32)
        qn = q * lax.rsqrt(_mean_sq_lanes(q, ones) + EPS) * q_gain
        kn = k * lax.rsqrt(_mean_sq_lanes(k, ones) + EPS) * k_gain
        ext = jnp.dot(pieces, place_ref[h], preferred_element_type=F32) + ext1_ref[...]
        qo_ref[h, :, 0:HEAD_DIM] = qn.astype(BF16)
        qo_ref[h, :, HEAD_DIM:2 * HEAD_DIM] = ext[:, 0:LANES].astype(BF16)
        ko_ref[h, :, 0:HEAD_DIM] = kn.astype(BF16)
        ko_ref[h, :, HEAD_DIM:2 * HEAD_DIM] = ext[:, LANES:2 * LANES].astype(BF16)


def _qk_prep(z3, c3, q_g, k_g):
    b, s, _ = z3.shape
    ts = _tile(s, 512)
    blk = (ATT_HEADS // 2) * HEAD_DIM
    qb0, kb0 = OFF_Q // blk, OFF_K // blk
    assert OFF_Q % blk == 0 and OFF_K % blk == 0
    out = jax.ShapeDtypeStruct((b, ATT_HEADS, s, 2 * HEAD_DIM), BF16)
    zspec = lambda cb: pl.BlockSpec((None, ts, blk), lambda i, j, cb=cb: (i, j, cb))
    ospec = pl.BlockSpec((None, ATT_HEADS, ts, 2 * HEAD_DIM), lambda i, j: (i, 0, j, 0))
    return pl.pallas_call(
        _qk_prep_kernel,
        grid=(b, s // ts),
        in_specs=[zspec(qb0), zspec(qb0 + 1), zspec(kb0), zspec(kb0 + 1),
                  pl.BlockSpec((None, ts, LANES), lambda i, j: (i, j, 0)),
                  pl.BlockSpec((1, HEAD_DIM), lambda i, j: (0, 0)),
                  pl.BlockSpec((1, HEAD_DIM), lambda i, j: (0, 0)),
                  pl.BlockSpec((ATT_HEADS, N_PIECES * LANES, 2 * LANES), lambda i, j: (0, 0, 0)),
                  pl.BlockSpec((1, 2 * LANES), lambda i, j: (0, 0)),
                  pl.BlockSpec((HEAD_DIM, HEAD_DIM), lambda i, j: (0, 0))],
        out_specs=[ospec, ospec],
        out_shape=[out, out],
        compiler_params=_params("parallel", "parallel"),
        name="qk_prep",
    )(z3, z3, z3, z3, c3, q_g, k_g, _placement(), _extension_ones(), jnp.ones((HEAD_DIM, HEAD_DIM), BF16))


ATT_HEADS_PER_STEP = 2


def _attn_kernel(q_ref, k_ref, v_ref, o_ref, *, tq):
    heads, s_len, _ = q_ref.shape
    for hh in range(heads):
        ls = slice(hh * HEAD_DIM, (hh + 1) * HEAD_DIM)
        for i in range(s_len // tq):
            n = (i + 1) * tq
            q = q_ref[hh, i * tq:(i + 1) * tq, :]
            s = lax.dot_general(q, k_ref[hh, 0:n, :], (((1,), (1,)), ((), ())), preferred_element_type=F32)
            row = lax.broadcasted_iota(jnp.int32, s.shape, 0) + i * tq
            col = lax.broadcasted_iota(jnp.int32, s.shape, 1)
            s = jnp.where(col <= row, s, NEG_BIG)
            m = jnp.max(s, axis=-1, keepdims=True)
            p = jnp.exp(s - m)
            l = jnp.sum(p, axis=-1, keepdims=True)
            acc = jnp.dot(p.astype(BF16), v_ref[0:n, ls], preferred_element_type=F32)
            o_ref[i * tq:(i + 1) * tq, ls] = (acc / l).astype(o_ref.dtype)


def _attention(qp, kp, z3):
    b, h, s, dk = qp.shape
    tq = _tile(s, 512)
    hs = ATT_HEADS_PER_STEP
    width = hs * HEAD_DIM
    assert h % hs == 0 and OFF_V % width == 0
    v0 = OFF_V // width
    return pl.pallas_call(
        functools.partial(_attn_kernel, tq=tq),
        grid=(b, h // hs),
        in_specs=[pl.BlockSpec((None, hs, s, dk), lambda bi, hi: (bi, hi, 0, 0)),
                  pl.BlockSpec((None, hs, s, dk), lambda bi, hi: (bi, hi, 0, 0)),
                  pl.BlockSpec((None, s, width), lambda bi, hi: (bi, 0, v0 + hi))],
        out_specs=pl.BlockSpec((None, s, width), lambda bi, hi: (bi, 0, hi)),
        out_shape=jax.ShapeDtypeStruct((b, s, ATT_WIDTH), BF16),
        compiler_params=_params("parallel", "parallel"),
        name="fox_attention",
    )(qp, kp, z3)


def _merge_kernel(h_ref, ya_ref, yb_ref, yc_ref, wa_ref, wb_ref, wc_ref,
                  wg0_ref, wg1_ref, wg2_ref, bg0_ref, bg1_ref, bg2_ref, o_ref):
    h = h_ref[...]

    def gated(y_ref, w_ref, wg_ref, bg_ref):
        gate = _sigmoid(jnp.dot(h, wg_ref[...], preferred_element_type=F32) + bg_ref[...])
        return gate * jnp.dot(y_ref[...], w_ref[...], preferred_element_type=F32)

    merged = (gated(ya_ref, wa_ref, wg0_ref, bg0_ref) + gated(yb_ref, wb_ref, wg1_ref, bg1_ref)
              + gated(yc_ref, wc_ref, wg2_ref, bg2_ref))
    o_ref[...] = merged.astype(o_ref.dtype)


def _merge(h, y_ab, y_c, w_br, w_gate, layer, b_gate, *, tm=1024, tn=512):
    m, d = h.shape
    tm, tn = _tile(m, tm), _tile(d, tn)
    nj = d // tn
    gate_w = lambda br: pl.BlockSpec((None, d, tn), lambda i, j, br=br: (layer, 0, br * nj + j))
    gate_b = lambda br: pl.BlockSpec((1, tn), lambda i, j, br=br: (0, br * nj + j))
    return pl.pallas_call(
        _merge_kernel,
        grid=(m // tm, nj),
        in_specs=[pl.BlockSpec((tm, d), lambda i, j: (i, 0)),
                  pl.BlockSpec((tm, CONV_WIDTH), lambda i, j: (i, 0)),
                  pl.BlockSpec((tm, POOL_WIDTH), lambda i, j: (i, 1)),
                  pl.BlockSpec((tm, ATT_WIDTH), lambda i, j: (i, 0)),
                  pl.BlockSpec((None, CONV_WIDTH, tn), lambda i, j: (layer, 0, j)),
                  pl.BlockSpec((None, POOL_WIDTH, tn), lambda i, j: (layer, 1, j)),
                  pl.BlockSpec((None, ATT_WIDTH, tn), lambda i, j: (layer, 1, j)),
                  gate_w(0), gate_w(1), gate_w(2), gate_b(0), gate_b(1), gate_b(2)],
        out_specs=pl.BlockSpec((tm, tn), lambda i, j: (i, j)),
        out_shape=jax.ShapeDtypeStruct((m, d), BF16),
        compiler_params=_params("parallel", "arbitrary"),
        name="gated_branch_merge",
    )(h, y_ab, y_ab, y_c, w_br, w_br, w_br, w_gate, w_gate, w_gate, b_gate, b_gate, b_gate)


def _out_proj_kernel(a_ref, w_ref, x_ref, o_ref):
    o_ref[...] = x_ref[...] + jnp.dot(a_ref[...], w_ref[...], preferred_element_type=F32)


def _out_proj(a, w, layer, x2, *, tm=1024, tn=1024):
    m, k = a.shape
    n = w.shape[2]
    tm, tn = _tile(m, tm), _tile(n, tn)
    return pl.pallas_call(
        _out_proj_kernel,
        grid=(m // tm, n // tn),
        in_specs=[pl.BlockSpec((tm, k), lambda i, j: (i, 0)),
                  pl.BlockSpec((None, k, tn), lambda i, j: (layer, 0, j)),
                  pl.BlockSpec((tm, tn), lambda i, j: (i, j))],
        out_specs=pl.BlockSpec((tm, tn), lambda i, j: (i, j)),
        out_shape=jax.ShapeDtypeStruct((m, n), F32),
        compiler_params=_params("parallel", "arbitrary"),
        name="out_proj_residual",
    )(a, w, x2)


def _swiglu_partial(h, wg, wu, wd):
    a = jnp.dot(h, wg, preferred_element_type=F32)
    u = jnp.dot(h, wu, preferred_element_type=F32)
    act = (a * _sigmoid(a) * u).astype(BF16)
    return jnp.dot(act, wd, preferred_element_type=F32)


def _ffn_kernel(x_ref, g_ref, wg_ref, wu_ref, wd_ref, o_ref, h_ref, acc_ref):
    f = pl.program_id(1)

    @pl.when(f == 0)
    def _():
        h_ref[...] = _rmsnorm_rows(x_ref[...], g_ref[...]).astype(BF16)
        acc_ref[...] = jnp.zeros(acc_ref.shape, F32)

    acc_ref[...] += _swiglu_partial(h_ref[...], wg_ref[...], wu_ref[...], wd_ref[...])

    @pl.when(f == pl.num_programs(1) - 1)
    def _():
        o_ref[...] = x_ref[...] + acc_ref[...]


def _ffn(x2, g, wg, wu, wd, layer, *, tm=512, tf=512):
    m, d = x2.shape
    ff = wg.shape[2]
    tm, tf = _tile(m, tm), _tile(ff, tf)
    return pl.pallas_call(
        _ffn_kernel,
        grid=(m // tm, ff // tf),
        in_specs=[pl.BlockSpec((tm, d), lambda i, f: (i, 0)),
                  pl.BlockSpec((1, d), lambda i, f: (0, 0)),
                  pl.BlockSpec((None, d, tf), lambda i, f: (layer, 0, f)),
                  pl.BlockSpec((None, d, tf), lambda i, f: (layer, 0, f)),
                  pl.BlockSpec((None, tf, d), lambda i, f: (layer, f, 0))],
        out_specs=pl.BlockSpec((tm, d), lambda i, f: (i, 0)),
        out_shape=jax.ShapeDtypeStruct((m, d), F32),
        scratch_shapes=[pltpu.VMEM((tm, d), BF16), pltpu.VMEM((tm, d), F32)],
        compiler_params=_params("parallel", "arbitrary"),
        name="swiglu_dense",
    )(x2, g, wg, wu, wd)


def _moe_ffn_kernel(te_ref, nu_ref, tok_ref, x_ref, g_ref, wg_ref, wu_ref, wd_ref, o_ref,
                    xbuf_ref, h_ref, acc_ref, sem, *, tm, rows_step):
    i = pl.program_id(0)
    f = pl.program_id(1)
    last = pl.num_programs(1) - 1
    n_used = nu_ref[0]
    used = i < n_used
    slot = i % 2
    rows_tile = xbuf_ref.shape[1]

    def row_copy(tile, r, slot_):
        return pltpu.make_async_copy(x_ref.at[pl.ds(tok_ref[tile * tm + r], 1)],
                                     xbuf_ref.at[slot_, pl.ds(r, 1)], sem.at[slot_])

    @pl.when((i == 0) & (f == 0))
    def _():
        def body(r, c):
            row_copy(0, r, 0).start()
            return c
        lax.fori_loop(0, rows_tile, body, 0, unroll=8)

    @pl.when((f == 0) & (i <= n_used))
    def _():
        def body(r, c):
            row_copy(i, r, slot).wait()
            return c
        lax.fori_loop(0, rows_tile, body, 0, unroll=8)

    @pl.when(used & (f == 0))
    def _():
        h_ref[...] = _rmsnorm_rows(xbuf_ref[slot, pl.ds(0, tm), :], g_ref[...]).astype(BF16)
        acc_ref[...] = jnp.zeros(acc_ref.shape, F32)

    @pl.when(used)
    def _():
        for r in range(rows_step):
            row_copy(i + 1, f * rows_step + r, 1 - slot).start()
        acc_ref[...] += _swiglu_partial(h_ref[...], wg_ref[...], wu_ref[...], wd_ref[...])

    @pl.when(f == last)
    def _():
        o_ref[...] = jnp.where(used, acc_ref[...], 0.0)


def _moe_ffn(tile_expert, n_used, src_tok, x2, g, wg, wu, wd, layer, *, tm, tf=512):
    mp = src_tok.shape[0]
    d = x2.shape[1]
    ff = wg.shape[3]
    tf = _tile(ff, tf)
    nf = ff // tf
    rows_step = -(-tm // (nf * SUBLANES)) * SUBLANES
    rows_tile = rows_step * nf
    src_tok = jnp.pad(src_tok, (0, 2 * rows_tile))

    def wspec(shape, fdim):
        def imap(i, f, te, nu, tok):
            fi = jnp.where(i < nu[0], f, nf - 1)
            return (layer, te[i], fi, 0) if fdim == 1 else (layer, te[i], 0, fi)
        return pl.BlockSpec(shape, imap)

    return pl.pallas_call(
        functools.partial(_moe_ffn_kernel, tm=tm, rows_step=rows_step),
        grid_spec=pltpu.PrefetchScalarGridSpec(
            num_scalar_prefetch=3,
            grid=(mp // tm, nf),
            in_specs=[pl.BlockSpec(memory_space=pl.ANY),
                      pl.BlockSpec((1, d), lambda i, f, te, nu, tok: (0, 0)),
                      wspec((None, None, d, tf), 2),
                      wspec((None, None, d, tf), 2),
                      wspec((None, None, tf, d), 1)],
            out_specs=pl.BlockSpec((tm, d), lambda i, f, te, nu, tok: (i, 0)),
            scratch_shapes=[pltpu.VMEM((2, rows_tile, d), F32), pltpu.VMEM((tm, d), BF16),
                            pltpu.VMEM((tm, d), F32), pltpu.SemaphoreType.DMA((2,))]),
        out_shape=jax.ShapeDtypeStruct((mp, d), F32),
        compiler_params=_params("arbitrary", "arbitrary"),
        name="swiglu_experts_grouped",
    )(tile_expert, n_used, src_tok, x2, g, wg, wu, wd)


def _combine_kernel(p1_ref, p2_ref, x_ref, sel_ref, y_ref, o_ref, buf_ref, sem, *, tc):
    base = pl.program_id(0) * tc

    def row_copy(src_row, slot, r):
        return pltpu.make_async_copy(y_ref.at[pl.ds(src_row, 1)], buf_ref.at[slot, pl.ds(r, 1)], sem)

    def issue(grp, c):
        r0 = pl.multiple_of(grp * SUBLANES, SUBLANES)
        for k in range(SUBLANES):
            row_copy(p1_ref[base + r0 + k], 0, r0 + k).start()
            row_copy(p2_ref[base + r0 + k], 1, r0 + k).start()
        return c

    def drain(r, c):
        row_copy(0, 0, r).wait()
        row_copy(0, 1, r).wait()
        return c

    lax.fori_loop(0, tc // SUBLANES, issue, 0)
    lax.fori_loop(0, tc, drain, 0, unroll=8)
    sel = sel_ref[...]
    w1, w2 = sel[:, TOP_K:TOP_K + 1], sel[:, TOP_K + 1:TOP_K + 2]
    o_ref[...] = x_ref[...] + w1 * buf_ref[0] + w2 * buf_ref[1]


def _combine(p1, p2, x2, sel, y, *, tc=512):
    m, d = x2.shape
    tc = _tile(m, tc)
    return pl.pallas_call(
        functools.partial(_combine_kernel, tc=tc),
        grid_spec=pltpu.PrefetchScalarGridSpec(
            num_scalar_prefetch=2,
            grid=(m // tc,),
            in_specs=[pl.BlockSpec((tc, d), lambda i, p1, p2: (i, 0)),
                      pl.BlockSpec((tc, LANES), lambda i, p1, p2: (i, 0)),
                      pl.BlockSpec(memory_space=pl.ANY)],
            out_specs=pl.BlockSpec((tc, d), lambda i, p1, p2: (i, 0)),
            scratch_shapes=[pltpu.VMEM((TOP_K, tc, d), F32), pltpu.SemaphoreType.DMA(())]),
        out_shape=jax.ShapeDtypeStruct((m, d), F32),
        compiler_params=_params("arbitrary"),
        name="combine_expert_rows",
    )(p1, p2, x2, sel, y)


def _route(sel, tm):
    m = sel.shape[0]
    e_flat = sel[:, :TOP_K].astype(jnp.int32).T.reshape(-1)
    onehot = (e_flat[:, None] == jnp.arange(N_EXPERTS, dtype=jnp.int32)[None, :]).astype(jnp.int32)
    csum = jnp.cumsum(onehot, axis=0)
    rank = jnp.sum((csum - onehot) * onehot, axis=1)
    counts = csum[-1]
    padded = ((counts + tm - 1) // tm) * tm
    ends = jnp.cumsum(padded)
    dest = (ends - padded)[e_flat] + rank
    mp = TOP_K * m + N_EXPERTS * tm
    n_tiles = mp // tm
    tok = jnp.tile(jnp.arange(m, dtype=jnp.int32), TOP_K)
    src_tok = jnp.zeros((mp,), jnp.int32).at[dest].set(tok)
    n_used = (ends[-1] // tm).astype(jnp.int32)
    tile_id = jnp.arange(n_tiles, dtype=jnp.int32)
    tile_e = jnp.sum((ends[None, :] <= (tile_id * tm)[:, None]).astype(jnp.int32), axis=1)
    tile_e = jnp.minimum(tile_e, N_EXPERTS - 1)
    tile_e = jnp.where(tile_id < n_used, tile_e, tile_e[jnp.maximum(n_used - 1, 0)])
    return src_tok, tile_e, n_used.reshape(1), dest[:m], dest[m:]


def _moe(x2, g, router_w, wg, wu, wd, layer):
    m, d = x2.shape
    tm = _tile(m, MOE_ROW_TILE)
    sel = _router(x2, g, router_w)
    src_tok, tile_e, n_used, p1, p2 = _route(sel, tm)
    y = _moe_ffn(tile_e, n_used, src_tok, x2, g, wg, wu, wd, layer, tm=tm)
    return _combine(p1, p2, x2, sel, y)


def _pad_cols(a, n):
    return jnp.pad(a, ((0, 0), (0, n - a.shape[1])))


def _mixing_sublayer(x2, bsz, seq, layer, norm_g, w_main, w_f, w_gate, b_f, b_gate, conv_w, conv_b, cln_g, cln_b,
                     pool_w, pool_scale, q_g, k_g, w_br, w_o):
    row = lambda v: v.reshape(1, -1)
    z, f_logit, h = _in_proj(x2, row(norm_g), w_main, w_f, layer, _pad_cols(row(b_f), LANES))

    z3 = z.reshape(bsz, seq, OFF_F)
    c3 = _log_forget_cumsum(f_logit.reshape(bsz, seq, LANES))
    y_ab = _mixers(z3, conv_w, row(conv_b), row(cln_g), row(cln_b), pool_w.astype(BF16), row(pool_scale))
    qp, kp = _qk_prep(z3, c3, row(q_g), row(k_g))
    y_c = _attention(qp, kp, z3)

    merged = _merge(h, y_ab.reshape(bsz * seq, -1), y_c.reshape(bsz * seq, -1), w_br, w_gate, layer, row(b_gate))
    return _out_proj(merged, w_o, layer, x2)


def kernel(x, norm1_g, w_in, b_f, b_gate, conv_w, conv_b, cln_g, cln_b, pool_w, pool_scale, q_g, k_g, w_br, w_o,
           norm2_g, ffn_wg, ffn_wu, ffn_wd, router, exp_wg, exp_wu, exp_wd):
    bsz, seq, d = x.shape
    depth = w_in.shape[0]
    x2 = x.reshape(bsz * seq, d)
    w_main = w_in[:, :, :OFF_F].astype(BF16)
    w_gate = w_in[:, :, OFF_G:].astype(BF16)
    w_f = jnp.pad(w_in[:, :, OFF_F:OFF_G], ((0, 0), (0, 0), (0, LANES - ATT_HEADS))).astype(BF16)
    w_br16, w_o16 = w_br.astype(BF16), w_o.astype(BF16)
    ffn16 = [w.astype(BF16) for w in (ffn_wg, ffn_wu, ffn_wd)]
    exp16 = [w.astype(BF16) for w in (exp_wg, exp_wu, exp_wd)]
    for l in range(depth):
        x2 = _mixing_sublayer(x2, bsz, seq, l, norm1_g[l], w_main, w_f, w_gate, b_f[l], b_gate[l],
                              conv_w[l], conv_b[l], cln_g[l], cln_b[l], pool_w[l], pool_scale[l],
                              q_g[l], k_g[l], w_br16, w_o16)
        g2 = norm2_g[l].reshape(1, d)
        j = l // 2
        if l % 2 == 0:
            x2 = _ffn(x2, g2, *ffn16, j)
        else:
            x2 = _moe(x2, g2, router[j], *exp16, j)
    return x2.reshape(bsz, seq, d)
```

```python
import functools

import jax
import jax.numpy as jnp
import numpy as np
from jax import lax
from jax.experimental import pallas as pl
from jax.experimental.pallas import tpu as pltpu

EPS = 1e-6
CONV_WIDTH = 512
CONV_K = 31
POOL_WIDTH = 512
POOL_WINDOWS = (2, 4, 8, 16)
POOL_GROUP = POOL_WIDTH // len(POOL_WINDOWS)
ATT_HEADS = 8
HEAD_DIM = 128
ATT_WIDTH = ATT_HEADS * HEAD_DIM
N_BRANCH = 3
N_EXPERTS = 8
TOP_K = 2

OFF_B = 2 * CONV_WIDTH
OFF_Q = OFF_B + POOL_WIDTH
OFF_K = OFF_Q + ATT_WIDTH
OFF_V = OFF_K + ATT_WIDTH
OFF_F = OFF_V + ATT_WIDTH
OFF_G = OFF_F + ATT_HEADS

LANES = 128
SUBLANES = 8
HALO = 32
NEG_BIG = -1e30
VMEM_LIMIT_BYTES = 56 * 1024 * 1024
MOE_ROW_TILE = 512

F32 = jnp.float32
BF16 = jnp.bfloat16


def _tile(n, pref):
    t = min(n, pref)
    while n % t:
        t -= 1
    return t


def _params(*sem):
    return pltpu.CompilerParams(dimension_semantics=sem, vmem_limit_bytes=VMEM_LIMIT_BYTES)


def _rmsnorm_rows(x, g):
    ms = jnp.mean(x * x, axis=-1, keepdims=True)
    return x * lax.rsqrt(ms + EPS) * g


def _sigmoid(x):
    return 1.0 / (1.0 + jnp.exp(-x))


def _log_sigmoid(x):
    return jnp.minimum(x, 0.0) - jnp.log1p(jnp.exp(-jnp.abs(x)))


def _top2(logits):
    lane = lax.broadcasted_iota(jnp.int32, logits.shape, 1).astype(F32)
    lg = jnp.where(lane < N_EXPERTS, logits, NEG_BIG)
    m1 = jnp.max(lg, axis=-1, keepdims=True)
    i1 = jnp.min(jnp.where(lg == m1, lane, float(LANES)), axis=-1, keepdims=True)
    lg2 = jnp.where(lane == i1, NEG_BIG, lg)
    m2 = jnp.max(lg2, axis=-1, keepdims=True)
    i2 = jnp.min(jnp.where(lg2 == m2, lane, float(LANES)), axis=-1, keepdims=True)
    e = jnp.exp(m2 - m1)
    w1 = 1.0 / (1.0 + e)
    w2 = e / (1.0 + e)
    return jnp.where(lane == 0, i1, jnp.where(lane == 1, i2, jnp.where(lane == 2, w1, jnp.where(lane == 3, w2, 0.0))))


def _split2(a):
    hi = a.astype(BF16)
    return hi, (a - hi.astype(F32)).astype(BF16)


def _router_kernel(x_ref, g_ref, whi_ref, wlo_ref, o_ref):
    h_hi, h_lo = _split2(_rmsnorm_rows(x_ref[...], g_ref[...]))
    logits = (jnp.dot(h_hi, whi_ref[...], preferred_element_type=F32)
              + jnp.dot(h_lo, whi_ref[...], preferred_element_type=F32)
              + jnp.dot(h_hi, wlo_ref[...], preferred_element_type=F32))
    o_ref[...] = _top2(logits)


def _router(x2, g, router_w, *, tm=1024):
    m, d = x2.shape
    tm = _tile(m, tm)
    w_hi, w_lo = _split2(_pad_cols(router_w, LANES))
    return pl.pallas_call(
        _router_kernel,
        grid=(m // tm,),
        in_specs=[pl.BlockSpec((tm, d), lambda i: (i, 0)),
                  pl.BlockSpec((1, d), lambda i: (0, 0)),
                  pl.BlockSpec((d, LANES), lambda i: (0, 0)),
                  pl.BlockSpec((d, LANES), lambda i: (0, 0))],
        out_specs=pl.BlockSpec((tm, LANES), lambda i: (i, 0)),
        out_shape=jax.ShapeDtypeStruct((m, LANES), F32),
        compiler_params=_params("parallel"),
        name="router_top2",
    )(x2, g, w_hi, w_lo)


def _in_proj_kernel(x_ref, g_ref, w_ref, wf_ref, bf_ref, o_ref, f_ref, h_ref):
    @pl.when(pl.program_id(1) == 0)
    def _():
        h = _rmsnorm_rows(x_ref[...], g_ref[...]).astype(BF16)
        h_ref[...] = h
        f_ref[...] = jnp.dot(h, wf_ref[...], preferred_element_type=F32) + bf_ref[...]

    o_ref[...] = jnp.dot(h_ref[...], w_ref[...], preferred_element_type=F32).astype(o_ref.dtype)


def _in_proj(x2, g, w, w_f, layer, b_f, *, tm=1024, tn=1536):
    m, d = x2.shape
    n = w.shape[2]
    tm, tn = _tile(m, tm), _tile(n, tn)
    return pl.pallas_call(
        _in_proj_kernel,
        grid=(m // tm, n // tn),
        in_specs=[pl.BlockSpec((tm, d), lambda i, j: (i, 0)),
                  pl.BlockSpec((1, d), lambda i, j: (0, 0)),
                  pl.BlockSpec((None, d, tn), lambda i, j: (layer, 0, j)),
                  pl.BlockSpec((None, d, LANES), lambda i, j: (layer, 0, 0)),
                  pl.BlockSpec((1, LANES), lambda i, j: (0, 0))],
        out_specs=[pl.BlockSpec((tm, tn), lambda i, j: (i, j)),
                   pl.BlockSpec((tm, LANES), lambda i, j: (i, 0)),
                   pl.BlockSpec((tm, d), lambda i, j: (i, 0))],
        out_shape=[jax.ShapeDtypeStruct((m, n), BF16), jax.ShapeDtypeStruct((m, LANES), F32),
                   jax.ShapeDtypeStruct((m, d), BF16)],
        compiler_params=_params("parallel", "arbitrary"),
        name="in_proj",
    )(x2, g, w, w_f, b_f)


def _cumsum_kernel(f_ref, o_ref, *, chunk):
    s = f_ref.shape[0]
    row = lax.broadcasted_iota(jnp.int32, (chunk, chunk), 0)
    col = lax.broadcasted_iota(jnp.int32, (chunk, chunk), 1)
    tril = (row >= col).astype(F32)
    carry = jnp.zeros((1, f_ref.shape[1]), F32)
    for c in range(s // chunk):
        ls = _log_sigmoid(f_ref[c * chunk:(c + 1) * chunk, :])
        cs = jnp.dot(tril, ls, preferred_element_type=F32, precision=lax.Precision.HIGHEST) + carry
        o_ref[c * chunk:(c + 1) * chunk, :] = cs
        carry = cs[chunk - 1:chunk, :]


def _log_forget_cumsum(f3):
    b, s, n = f3.shape
    chunk = _tile(s, LANES)
    return pl.pallas_call(
        functools.partial(_cumsum_kernel, chunk=chunk),
        grid=(b,),
        in_specs=[pl.BlockSpec((None, s, n), lambda i: (i, 0, 0))],
        out_specs=pl.BlockSpec((None, s, n), lambda i: (i, 0, 0)),
        out_shape=jax.ShapeDtypeStruct((b, s, n), F32),
        compiler_params=_params("parallel"),
        name="log_forget_cumsum",
    )(f3)


def _shift_rows(win, p):
    if p == 0:
        return win
    return pltpu.roll(win, win.shape[0] - p, axis=0)


def _mixers_kernel(z_ref, cw_ref, cb_ref, lg_ref, lb_ref, pw_ref, ps_ref, o_ref, zp_ref, up_ref, cv_ref, *, rows):
    s = z_ref.shape[0]
    n_chunks = s // rows
    zp_ref[0:HALO, :] = jnp.zeros((HALO, CONV_WIDTH), F32)
    up_ref[0:HALO, :] = jnp.zeros((HALO, POOL_WIDTH), F32)

    def fill(c, _):
        r0 = pl.multiple_of(c * rows, rows)
        a = z_ref[pl.ds(r0, rows), 0:CONV_WIDTH].astype(F32)
        gt = z_ref[pl.ds(r0, rows), CONV_WIDTH:OFF_B].astype(F32)
        zp_ref[pl.ds(r0 + HALO, rows), :] = a * _sigmoid(gt)
        up_ref[pl.ds(r0 + HALO, rows), :] = z_ref[pl.ds(r0, rows), OFF_B:OFF_Q].astype(F32)
        return 0

    lax.fori_loop(0, n_chunks, fill, 0)

    lead = HALO - (CONV_K - 1)

    def mix(c, _):
        r0 = pl.multiple_of(c * rows, rows)
        for lg in range(CONV_WIDTH // LANES):
            ls = slice(lg * LANES, (lg + 1) * LANES)
            win = zp_ref[pl.ds(r0, rows + HALO), ls]
            acc = jnp.zeros((rows, LANES), F32) + cb_ref[:, ls]
            for p in range(SUBLANES):
                wp = _shift_rows(win, p)
                for k in range(CONV_K):
                    if (lead + k) % SUBLANES == p:
                        off = lead + k - p
                        acc = acc + cw_ref[k:k + 1, ls] * wp[off:off + rows, :]
            cv_ref[:, ls] = acc
        y = cv_ref[...]
        mu = jnp.mean(y, axis=-1, keepdims=True)
        yc = y - mu
        var = jnp.mean(yc * yc, axis=-1, keepdims=True)
        y = yc * lax.rsqrt(var + EPS) * lg_ref[...] + lb_ref[...]
        o_ref[pl.ds(r0, rows), 0:CONV_WIDTH] = (y * _sigmoid(y)).astype(o_ref.dtype)

        half = HALO // 2
        pos = (lax.broadcasted_iota(jnp.int32, (rows, LANES), 0) + (r0 + 1)).astype(F32)
        for gi, w in enumerate(POOL_WINDOWS):
            ls = slice(gi * POOL_GROUP, (gi + 1) * POOL_GROUP)
            win = up_ref[pl.ds(r0 + half, rows + half), ls]
            tot = win
            span = 1
            while span < w:
                tot = tot + pltpu.roll(tot, span, axis=0)
                span *= 2
            u = win[half:half + rows, :]
            p = tot[half:half + rows, :] / jnp.minimum(pos, float(w)) - u
            yb = jnp.dot(p.astype(BF16), pw_ref[gi], preferred_element_type=F32) * ps_ref[:, ls]
            o_ref[pl.ds(r0, rows), CONV_WIDTH + gi * POOL_GROUP:CONV_WIDTH + (gi + 1) * POOL_GROUP] = yb.astype(o_ref.dtype)
        return 0

    lax.fori_loop(0, n_chunks, mix, 0)


def _mixers(z3, conv_w, conv_b, cln_g, cln_b, pool_w, pool_scale):
    b, s, n = z3.shape
    rows = _tile(s, 256)
    width = OFF_Q
    assert n % width == 0
    return pl.pallas_call(
        functools.partial(_mixers_kernel, rows=rows),
        grid=(b,),
        in_specs=[pl.BlockSpec((None, s, width), lambda i: (i, 0, 0)),
                  pl.BlockSpec(conv_w.shape, lambda i: (0, 0)),
                  pl.BlockSpec(conv_b.shape, lambda i: (0, 0)),
                  pl.BlockSpec(cln_g.shape, lambda i: (0, 0)),
                  pl.BlockSpec(cln_b.shape, lambda i: (0, 0)),
                  pl.BlockSpec(pool_w.shape, lambda i: (0, 0, 0)),
                  pl.BlockSpec(pool_scale.shape, lambda i: (0, 0))],
        out_specs=pl.BlockSpec((None, s, CONV_WIDTH + POOL_WIDTH), lambda i: (i, 0, 0)),
        out_shape=jax.ShapeDtypeStruct((b, s, CONV_WIDTH + POOL_WIDTH), BF16),
        scratch_shapes=[pltpu.VMEM((s + HALO, CONV_WIDTH), F32),
                        pltpu.VMEM((s + HALO, POOL_WIDTH), F32),
                        pltpu.VMEM((rows, CONV_WIDTH), F32)],
        compiler_params=_params("parallel"),
        name="conv_pool_mixers",
    )(z3, conv_w, conv_b, cln_g, cln_b, pool_w, pool_scale)


def _split3(c):
    hi = c.astype(BF16).astype(F32)
    r1 = c - hi
    mid = r1.astype(BF16).astype(F32)
    return hi, mid, r1 - mid


N_PIECES = 3


def _placement():
    e = np.zeros((ATT_HEADS, N_PIECES * LANES, 2 * LANES), np.float32)
    for h in range(ATT_HEADS):
        for p in range(N_PIECES):
            e[h, p * LANES + h, p] = 1.0
            e[h, p * LANES + h, LANES + N_PIECES + p] = -1.0
    return jnp.asarray(e, BF16)


def _extension_ones():
    o = np.zeros((1, 2 * LANES), np.float32)
    o[0, N_PIECES:2 * N_PIECES] = 1.0
    o[0, LANES:LANES + N_PIECES] = 1.0
    return jnp.asarray(o)


def _mean_sq_lanes(x, ones):
    hi, lo = _split2(x * x)
    tot = jnp.dot(hi, ones, preferred_element_type=F32) + jnp.dot(lo, ones, preferred_element_type=F32)
    return tot * (1.0 / HEAD_DIM)


def _qk_prep_kernel(qa_ref, qb_ref, ka_ref, kb_ref, c_ref, qg_ref, kg_ref, place_ref, ext1_ref, ones_ref,
                    qo_ref, ko_ref):
    scale = HEAD_DIM ** -0.5
    pieces = jnp.concatenate(_split3(c_ref[...]), axis=1).astype(BF16)
    ones = ones_ref[...]
    q_gain = qg_ref[...] * scale
    k_gain = kg_ref[...]
    half = ATT_HEADS // 2
    for h in range(ATT_HEADS):
        q_src, k_src = (qa_ref, ka_ref) if h < half else (qb_ref, kb_ref)
        ls = slice((h % half) * HEAD_DIM, (h % half + 1) * HEAD_DIM)
        q = q_src[:, ls].astype(F32)
        k = k_src[:, ls].astype(F32)
        qn = q * lax.rsqrt(_mean_sq_lanes(q, ones) + EPS) * q_gain
        kn = k * lax.rsqrt(_mean_sq_lanes(k, ones) + EPS) * k_gain
        ext = jnp.dot(pieces, place_ref[h], preferred_element_type=F32) + ext1_ref[...]
        qo_ref[h, :, 0:HEAD_DIM] = qn.astype(BF16)
        qo_ref[h, :, HEAD_DIM:2 * HEAD_DIM] = ext[:, 0:LANES].astype(BF16)
        ko_ref[h, :, 0:HEAD_DIM] = kn.astype(BF16)
        ko_ref[h, :, HEAD_DIM:2 * HEAD_DIM] = ext[:, LANES:2 * LANES].astype(BF16)


def _qk_prep(z3, c3, q_g, k_g):
    b, s, _ = z3.shape
    ts = _tile(s, 512)
    blk = (ATT_HEADS // 2) * HEAD_DIM
    qb0, kb0 = OFF_Q // blk, OFF_K // blk
    assert OFF_Q % blk == 0 and OFF_K % blk == 0
    out = jax.ShapeDtypeStruct((b, ATT_HEADS, s, 2 * HEAD_DIM), BF16)
    zspec = lambda cb: pl.BlockSpec((None, ts, blk), lambda i, j, cb=cb: (i, j, cb))
    ospec = pl.BlockSpec((None, ATT_HEADS, ts, 2 * HEAD_DIM), lambda i, j: (i, 0, j, 0))
    return pl.pallas_call(
        _qk_prep_kernel,
        grid=(b, s // ts),
        in_specs=[zspec(qb0), zspec(qb0 + 1), zspec(kb0), zspec(kb0 + 1),
                  pl.BlockSpec((None, ts, LANES), lambda i, j: (i, j, 0)),
                  pl.BlockSpec((1, HEAD_DIM), lambda i, j: (0, 0)),
                  pl.BlockSpec((1, HEAD_DIM), lambda i, j: (0, 0)),
                  pl.BlockSpec((ATT_HEADS, N_PIECES * LANES, 2 * LANES), lambda i, j: (0, 0, 0)),
                  pl.BlockSpec((1, 2 * LANES), lambda i, j: (0, 0)),
                  pl.BlockSpec((HEAD_DIM, HEAD_DIM), lambda i, j: (0, 0))],
        out_specs=[ospec, ospec],
        out_shape=[out, out],
        compiler_params=_params("parallel", "parallel"),
        name="qk_prep",
    )(z3, z3, z3, z3, c3, q_g, k_g, _placement(), _extension_ones(), jnp.ones((HEAD_DIM, HEAD_DIM), BF16))


ATT_HEADS_PER_STEP = 2


def _attn_kernel(q_ref, k_ref, v_ref, o_ref, *, tq):
    heads, s_len, _ = q_ref.shape
    for hh in range(heads):
        ls = slice(hh * HEAD_DIM, (hh + 1) * HEAD_DIM)
        for i in range(s_len // tq):
            n = (i + 1) * tq
            q = q_ref[hh, i * tq:(i + 1) * tq, :]
            s = lax.dot_general(q, k_ref[hh, 0:n, :], (((1,), (1,)), ((), ())), preferred_element_type=F32)
            row = lax.broadcasted_iota(jnp.int32, s.shape, 0) + i * tq
            col = lax.broadcasted_iota(jnp.int32, s.shape, 1)
            s = jnp.where(col <= row, s, NEG_BIG)
            m = jnp.max(s, axis=-1, keepdims=True)
            p = jnp.exp(s - m)
            l = jnp.sum(p, axis=-1, keepdims=True)
            acc = jnp.dot(p.astype(BF16), v_ref[0:n, ls], preferred_element_type=F32)
            o_ref[i * tq:(i + 1) * tq, ls] = (acc / l).astype(o_ref.dtype)


def _attention(qp, kp, z3):
    b, h, s, dk = qp.shape
    tq = _tile(s, 512)
    hs = ATT_HEADS_PER_STEP
    width = hs * HEAD_DIM
    assert h % hs == 0 and OFF_V % width == 0
    v0 = OFF_V // width
    return pl.pallas_call(
        functools.partial(_attn_kernel, tq=tq),
        grid=(b, h // hs),
        in_specs=[pl.BlockSpec((None, hs, s, dk), lambda bi, hi: (bi, hi, 0, 0)),
                  pl.BlockSpec((None, hs, s, dk), lambda bi, hi: (bi, hi, 0, 0)),
                  pl.BlockSpec((None, s, width), lambda bi, hi: (bi, 0, v0 + hi))],
        out_specs=pl.BlockSpec((None, s, width), lambda bi, hi: (bi, 0, hi)),
        out_shape=jax.ShapeDtypeStruct((b, s, ATT_WIDTH), BF16),
        compiler_params=_params("parallel", "parallel"),
        name="fox_attention",
    )(qp, kp, z3)


def _merge_kernel(h_ref, ya_ref, yb_ref, yc_ref, wa_ref, wb_ref, wc_ref,
                  wg0_ref, wg1_ref, wg2_ref, bg0_ref, bg1_ref, bg2_ref, o_ref):
    h = h_ref[...]

    def gated(y_ref, w_ref, wg_ref, bg_ref):
        gate = _sigmoid(jnp.dot(h, wg_ref[...], preferred_element_type=F32) + bg_ref[...])
        return gate * jnp.dot(y_ref[...], w_ref[...], preferred_element_type=F32)

    merged = (gated(ya_ref, wa_ref, wg0_ref, bg0_ref) + gated(yb_ref, wb_ref, wg1_ref, bg1_ref)
              + gated(yc_ref, wc_ref, wg2_ref, bg2_ref))
    o_ref[...] = merged.astype(o_ref.dtype)


def _merge(h, y_ab, y_c, w_br, w_gate, layer, b_gate, *, tm=1024, tn=512):
    m, d = h.shape
    tm, tn = _tile(m, tm), _tile(d, tn)
    nj = d // tn
    gate_w = lambda br: pl.BlockSpec((None, d, tn), lambda i, j, br=br: (layer, 0, br * nj + j))
    gate_b = lambda br: pl.BlockSpec((1, tn), lambda i, j, br=br: (0, br * nj + j))
    return pl.pallas_call(
        _merge_kernel,
        grid=(m // tm, nj),
        in_specs=[pl.BlockSpec((tm, d), lambda i, j: (i, 0)),
                  pl.BlockSpec((tm, CONV_WIDTH), lambda i, j: (i, 0)),
                  pl.BlockSpec((tm, POOL_WIDTH), lambda i, j: (i, 1)),
                  pl.BlockSpec((tm, ATT_WIDTH), lambda i, j: (i, 0)),
                  pl.BlockSpec((None, CONV_WIDTH, tn), lambda i, j: (layer, 0, j)),
                  pl.BlockSpec((None, POOL_WIDTH, tn), lambda i, j: (layer, 1, j)),
                  pl.BlockSpec((None, ATT_WIDTH, tn), lambda i, j: (layer, 1, j)),
                  gate_w(0), gate_w(1), gate_w(2), gate_b(0), gate_b(1), gate_b(2)],
        out_specs=pl.BlockSpec((tm, tn), lambda i, j: (i, j)),
        out_shape=jax.ShapeDtypeStruct((m, d), BF16),
        compiler_params=_params("parallel", "arbitrary"),
        name="gated_branch_merge",
    )(h, y_ab, y_ab, y_c, w_br, w_br, w_br, w_gate, w_gate, w_gate, b_gate, b_gate, b_gate)


def _out_proj_kernel(a_ref, w_ref, x_ref, o_ref):
    o_ref[...] = x_ref[...] + jnp.dot(a_ref[...], w_ref[...], preferred_element_type=F32)


def _out_proj(a, w, layer, x2, *, tm=1024, tn=1024):
    m, k = a.shape
    n = w.shape[2]
    tm, tn = _tile(m, tm), _tile(n, tn)
    return pl.pallas_call(
        _out_proj_kernel,
        grid=(m // tm, n // tn),
        in_specs=[pl.BlockSpec((tm, k), lambda i, j: (i, 0)),
                  pl.BlockSpec((None, k, tn), lambda i, j: (layer, 0, j)),
                  pl.BlockSpec((tm, tn), lambda i, j: (i, j))],
        out_specs=pl.BlockSpec((tm, tn), lambda i, j: (i, j)),
        out_shape=jax.ShapeDtypeStruct((m, n), F32),
        compiler_params=_params("parallel", "arbitrary"),
        name="out_proj_residual",
    )(a, w, x2)


def _swiglu_partial(h, wg, wu, wd):
    a = jnp.dot(h, wg, preferred_element_type=F32)
    u = jnp.dot(h, wu, preferred_element_type=F32)
    act = (a * _sigmoid(a) * u).astype(BF16)
    return jnp.dot(act, wd, preferred_element_type=F32)


def _ffn_kernel(x_ref, g_ref, wg_ref, wu_ref, wd_ref, o_ref, h_ref):
    @pl.when(pl.program_id(1) == 0)
    def _():
        x = x_ref[...]
        h_ref[...] = _rmsnorm_rows(x, g_ref[...]).astype(BF16)
        o_ref[...] = x

    o_ref[...] += _swiglu_partial(h_ref[...], wg_ref[...], wu_ref[...], wd_ref[...])


def _ffn(x2, g, wg, wu, wd, layer, *, tm=1024, tf=512):
    m, d = x2.shape
    ff = wg.shape[2]
    tm, tf = _tile(m, tm), _tile(ff, tf)
    return pl.pallas_call(
        _ffn_kernel,
        grid=(m // tm, ff // tf),
        in_specs=[pl.BlockSpec((tm, d), lambda i, f: (i, 0)),
                  pl.BlockSpec((1, d), lambda i, f: (0, 0)),
                  pl.BlockSpec((None, d, tf), lambda i, f: (layer, 0, f)),
                  pl.BlockSpec((None, d, tf), lambda i, f: (layer, 0, f)),
                  pl.BlockSpec((None, tf, d), lambda i, f: (layer, f, 0))],
        out_specs=pl.BlockSpec((tm, d), lambda i, f: (i, 0)),
        out_shape=jax.ShapeDtypeStruct((m, d), F32),
        scratch_shapes=[pltpu.VMEM((tm, d), BF16)],
        compiler_params=_params("parallel", "arbitrary"),
        name="swiglu_dense",
    )(x2, g, wg, wu, wd)


def _moe_ffn_kernel(te_ref, nu_ref, tok_ref, x_ref, g_ref, wg_ref, wu_ref, wd_ref, o_ref,
                    xbuf_ref, h_ref, acc_ref, sem, *, tm, rows_step):
    i = pl.program_id(0)
    f = pl.program_id(1)
    last = pl.num_programs(1) - 1
    n_used = nu_ref[0]
    used = i < n_used
    slot = i % 2
    rows_tile = xbuf_ref.shape[1]

    def row_copy(tile, r, slot_):
        return pltpu.make_async_copy(x_ref.at[pl.ds(tok_ref[tile * tm + r], 1)],
                                     xbuf_ref.at[slot_, pl.ds(r, 1)], sem.at[slot_])

    @pl.when((i == 0) & (f == 0))
    def _():
        def body(r, c):
            row_copy(0, r, 0).start()
            return c
        lax.fori_loop(0, rows_tile, body, 0, unroll=8)

    @pl.when((f == 0) & (i <= n_used))
    def _():
        def body(r, c):
            row_copy(i, r, slot).wait()
            return c
        lax.fori_loop(0, rows_tile, body, 0, unroll=8)

    @pl.when(used & (f == 0))
    def _():
        h_ref[...] = _rmsnorm_rows(xbuf_ref[slot, pl.ds(0, tm), :], g_ref[...]).astype(BF16)
        acc_ref[...] = jnp.zeros(acc_ref.shape, F32)

    @pl.when(used)
    def _():
        for r in range(rows_step):
            row_copy(i + 1, f * rows_step + r, 1 - slot).start()
        acc_ref[...] += _swiglu_partial(h_ref[...], wg_ref[...], wu_ref[...], wd_ref[...])

    @pl.when(f == last)
    def _():
        o_ref[...] = jnp.where(used, acc_ref[...], 0.0)


def _moe_ffn(tile_expert, n_used, src_tok, x2, g, wg, wu, wd, layer, *, tm, tf=512):
    mp = src_tok.shape[0]
    d = x2.shape[1]
    ff = wg.shape[3]
    tf = _tile(ff, tf)
    nf = ff // tf
    rows_step = -(-tm // (nf * SUBLANES)) * SUBLANES
    rows_tile = rows_step * nf
    src_tok = jnp.pad(src_tok, (0, 2 * rows_tile))

    def wspec(shape, fdim):
        def imap(i, f, te, nu, tok):
            fi = jnp.where(i < nu[0], f, nf - 1)
            return (layer, te[i], fi, 0) if fdim == 1 else (layer, te[i], 0, fi)
        return pl.BlockSpec(shape, imap)

    return pl.pallas_call(
        functools.partial(_moe_ffn_kernel, tm=tm, rows_step=rows_step),
        grid_spec=pltpu.PrefetchScalarGridSpec(
            num_scalar_prefetch=3,
            grid=(mp // tm, nf),
            in_specs=[pl.BlockSpec(memory_space=pl.ANY),
                      pl.BlockSpec((1, d), lambda i, f, te, nu, tok: (0, 0)),
                      wspec((None, None, d, tf), 2),
                      wspec((None, None, d, tf), 2),
                      wspec((None, None, tf, d), 1)],
            out_specs=pl.BlockSpec((tm, d), lambda i, f, te, nu, tok: (i, 0)),
            scratch_shapes=[pltpu.VMEM((2, rows_tile, d), F32), pltpu.VMEM((tm, d), BF16),
                            pltpu.VMEM((tm, d), F32), pltpu.SemaphoreType.DMA((2,))]),
        out_shape=jax.ShapeDtypeStruct((mp, d), F32),
        compiler_params=_params("arbitrary", "arbitrary"),
        name="swiglu_experts_grouped",
    )(tile_expert, n_used, src_tok, x2, g, wg, wu, wd)


def _combine_kernel(p1_ref, p2_ref, x_ref, sel_ref, y_ref, o_ref, buf_ref, sem, *, tc):
    base = pl.program_id(0) * tc

    def row_copy(src_row, slot, r):
        return pltpu.make_async_copy(y_ref.at[pl.ds(src_row, 1)], buf_ref.at[slot, pl.ds(r, 1)], sem)

    def issue(grp, c):
        r0 = pl.multiple_of(grp * SUBLANES, SUBLANES)
        for k in range(SUBLANES):
            row_copy(p1_ref[base + r0 + k], 0, r0 + k).start()
            row_copy(p2_ref[base + r0 + k], 1, r0 + k).start()
        return c

    def drain(r, c):
        row_copy(0, 0, r).wait()
        row_copy(0, 1, r).wait()
        return c

    lax.fori_loop(0, tc // SUBLANES, issue, 0)
    lax.fori_loop(0, tc, drain, 0, unroll=8)
    sel = sel_ref[...]
    w1, w2 = sel[:, TOP_K:TOP_K + 1], sel[:, TOP_K + 1:TOP_K + 2]
    o_ref[...] = x_ref[...] + w1 * buf_ref[0] + w2 * buf_ref[1]


def _combine(p1, p2, x2, sel, y, *, tc=512):
    m, d = x2.shape
    tc = _tile(m, tc)
    return pl.pallas_call(
        functools.partial(_combine_kernel, tc=tc),
        grid_spec=pltpu.PrefetchScalarGridSpec(
            num_scalar_prefetch=2,
            grid=(m // tc,),
            in_specs=[pl.BlockSpec((tc, d), lambda i, p1, p2: (i, 0)),
                      pl.BlockSpec((tc, LANES), lambda i, p1, p2: (i, 0)),
                      pl.BlockSpec(memory_space=pl.ANY)],
            out_specs=pl.BlockSpec((tc, d), lambda i, p1, p2: (i, 0)),
            scratch_shapes=[pltpu.VMEM((TOP_K, tc, d), F32), pltpu.SemaphoreType.DMA(())]),
        out_shape=jax.ShapeDtypeStruct((m, d), F32),
        compiler_params=_params("arbitrary"),
        name="combine_expert_rows",
    )(p1, p2, x2, sel, y)


def _route(sel, tm):
    m = sel.shape[0]
    e_flat = sel[:, :TOP_K].astype(jnp.int32).T.reshape(-1)
    onehot = (e_flat[:, None] == jnp.arange(N_EXPERTS, dtype=jnp.int32)[None, :]).astype(jnp.int32)
    csum = jnp.cumsum(onehot, axis=0)
    rank = jnp.sum((csum - onehot) * onehot, axis=1)
    counts = csum[-1]
    padded = ((counts + tm - 1) // tm) * tm
    ends = jnp.cumsum(padded)
    dest = (ends - padded)[e_flat] + rank
    mp = TOP_K * m + N_EXPERTS * tm
    n_tiles = mp // tm
    tok = jnp.tile(jnp.arange(m, dtype=jnp.int32), TOP_K)
    src_tok = jnp.zeros((mp,), jnp.int32).at[dest].set(tok)
    n_used = (ends[-1] // tm).astype(jnp.int32)
    tile_id = jnp.arange(n_tiles, dtype=jnp.int32)
    tile_e = jnp.sum((ends[None, :] <= (tile_id * tm)[:, None]).astype(jnp.int32), axis=1)
    tile_e = jnp.minimum(tile_e, N_EXPERTS - 1)
    tile_e = jnp.where(tile_id < n_used, tile_e, tile_e[jnp.maximum(n_used - 1, 0)])
    return src_tok, tile_e, n_used.reshape(1), dest[:m], dest[m:]


def _moe(x2, g, router_w, wg, wu, wd, layer):
    m, d = x2.shape
    tm = _tile(m, MOE_ROW_TILE)
    sel = _router(x2, g, router_w)
    src_tok, tile_e, n_used, p1, p2 = _route(sel, tm)
    y = _moe_ffn(tile_e, n_used, src_tok, x2, g, wg, wu, wd, layer, tm=tm)
    return _combine(p1, p2, x2, sel, y)


def _pad_cols(a, n):
    return jnp.pad(a, ((0, 0), (0, n - a.shape[1])))


def _mixing_sublayer(x2, bsz, seq, layer, norm_g, w_main, w_f, w_gate, b_f, b_gate, conv_w, conv_b, cln_g, cln_b,
                     pool_w, pool_scale, q_g, k_g, w_br, w_o):
    row = lambda v: v.reshape(1, -1)
    z, f_logit, h = _in_proj(x2, row(norm_g), w_main, w_f, layer, _pad_cols(row(b_f), LANES))

    z3 = z.reshape(bsz, seq, OFF_F)
    c3 = _log_forget_cumsum(f_logit.reshape(bsz, seq, LANES))
    y_ab = _mixers(z3, conv_w, row(conv_b), row(cln_g), row(cln_b), pool_w.astype(BF16), row(pool_scale))
    qp, kp = _qk_prep(z3, c3, row(q_g), row(k_g))
    y_c = _attention(qp, kp, z3)

    merged = _merge(h, y_ab.reshape(bsz * seq, -1), y_c.reshape(bsz * seq, -1), w_br, w_gate, layer, row(b_gate))
    return _out_proj(merged, w_o, layer, x2)


def kernel(x, norm1_g, w_in, b_f, b_gate, conv_w, conv_b, cln_g, cln_b, pool_w, pool_scale, q_g, k_g, w_br, w_o,
           norm2_g, ffn_wg, ffn_wu, ffn_wd, router, exp_wg, exp_wu, exp_wd):
    bsz, seq, d = x.shape
    depth = w_in.shape[0]
    x2 = x.reshape(bsz * seq, d)
    w_main = w_in[:, :, :OFF_F].astype(BF16)
    w_gate = w_in[:, :, OFF_G:].astype(BF16)
    w_f = jnp.pad(w_in[:, :, OFF_F:OFF_G], ((0, 0), (0, 0), (0, LANES - ATT_HEADS))).astype(BF16)
    w_br16, w_o16 = w_br.astype(BF16), w_o.astype(BF16)
    ffn16 = [w.astype(BF16) for w in (ffn_wg, ffn_wu, ffn_wd)]
    exp16 = [w.astype(BF16) for w in (exp_wg, exp_wu, exp_wd)]
    for l in range(depth):
        x2 = _mixing_sublayer(x2, bsz, seq, l, norm1_g[l], w_main, w_f, w_gate, b_f[l], b_gate[l],
                              conv_w[l], conv_b[l], cln_g[l], cln_b[l], pool_w[l], pool_scale[l],
                              q_g[l], k_g[l], w_br16, w_o16)
        g2 = norm2_g[l].reshape(1, d)
        j = l // 2
        if l % 2 == 0:
            x2 = _ffn(x2, g2, *ffn16, j)
        else:
            x2 = _moe(x2, g2, router[j], *exp16, j)
    return x2.reshape(bsz, seq, d)
```

```python
import functools

import jax
import jax.numpy as jnp
import numpy as np
from jax import lax
from jax.experimental import pallas as pl
from jax.experimental.pallas import tpu as pltpu

EPS = 1e-6
CONV_WIDTH = 512
CONV_K = 31
POOL_WIDTH = 512
POOL_WINDOWS = (2, 4, 8, 16)
POOL_GROUP = POOL_WIDTH // len(POOL_WINDOWS)
ATT_HEADS = 8
HEAD_DIM = 128
ATT_WIDTH = ATT_HEADS * HEAD_DIM
N_BRANCH = 3
N_EXPERTS = 8
TOP_K = 2

OFF_B = 2 * CONV_WIDTH
OFF_Q = OFF_B + POOL_WIDTH
OFF_K = OFF_Q + ATT_WIDTH
OFF_V = OFF_K + ATT_WIDTH
OFF_F = OFF_V + ATT_WIDTH
OFF_G = OFF_F + ATT_HEADS

LANES = 128
SUBLANES = 8
HALO = 32
NEG_BIG = -1e30
VMEM_LIMIT_BYTES = 56 * 1024 * 1024
MOE_ROW_TILE = 1024

F32 = jnp.float32
BF16 = jnp.bfloat16


def _tile(n, pref):
    t = min(n, pref)
    while n % t:
        t -= 1
    return t


def _params(*sem):
    return pltpu.CompilerParams(dimension_semantics=sem, vmem_limit_bytes=VMEM_LIMIT_BYTES)


def _rmsnorm_rows(x, g):
    ms = jnp.mean(x * x, axis=-1, keepdims=True)
    return x * lax.rsqrt(ms + EPS) * g


def _sigmoid(x):
    return 1.0 / (1.0 + jnp.exp(-x))


def _log_sigmoid(x):
    return jnp.minimum(x, 0.0) - jnp.log1p(jnp.exp(-jnp.abs(x)))


def _top2(logits):
    lane = lax.broadcasted_iota(jnp.int32, logits.shape, 1).astype(F32)
    lg = jnp.where(lane < N_EXPERTS, logits, NEG_BIG)
    m1 = jnp.max(lg, axis=-1, keepdims=True)
    i1 = jnp.min(jnp.where(lg == m1, lane, float(LANES)), axis=-1, keepdims=True)
    lg2 = jnp.where(lane == i1, NEG_BIG, lg)
    m2 = jnp.max(lg2, axis=-1, keepdims=True)
    i2 = jnp.min(jnp.where(lg2 == m2, lane, float(LANES)), axis=-1, keepdims=True)
    e = jnp.exp(m2 - m1)
    w1 = 1.0 / (1.0 + e)
    w2 = e / (1.0 + e)
    return jnp.where(lane == 0, i1, jnp.where(lane == 1, i2, jnp.where(lane == 2, w1, jnp.where(lane == 3, w2, 0.0))))


def _split2(a):
    hi = a.astype(BF16)
    return hi, (a - hi.astype(F32)).astype(BF16)


def _router_kernel(x_ref, g_ref, whi_ref, wlo_ref, o_ref):
    h_hi, h_lo = _split2(_rmsnorm_rows(x_ref[...], g_ref[...]))
    logits = (jnp.dot(h_hi, whi_ref[...], preferred_element_type=F32)
              + jnp.dot(h_lo, whi_ref[...], preferred_element_type=F32)
              + jnp.dot(h_hi, wlo_ref[...], preferred_element_type=F32))
    o_ref[...] = _top2(logits)


def _router(x2, g, router_w, *, tm=1024):
    m, d = x2.shape
    tm = _tile(m, tm)
    w_hi, w_lo = _split2(_pad_cols(router_w, LANES))
    return pl.pallas_call(
        _router_kernel,
        grid=(m // tm,),
        in_specs=[pl.BlockSpec((tm, d), lambda i: (i, 0)),
                  pl.BlockSpec((1, d), lambda i: (0, 0)),
                  pl.BlockSpec((d, LANES), lambda i: (0, 0)),
                  pl.BlockSpec((d, LANES), lambda i: (0, 0))],
        out_specs=pl.BlockSpec((tm, LANES), lambda i: (i, 0)),
        out_shape=jax.ShapeDtypeStruct((m, LANES), F32),
        compiler_params=_params("parallel"),
        name="router_top2",
    )(x2, g, w_hi, w_lo)


def _in_proj_kernel(x_ref, g_ref, w_ref, wf_ref, bf_ref, o_ref, f_ref, h_ref):
    @pl.when(pl.program_id(1) == 0)
    def _():
        h = _rmsnorm_rows(x_ref[...], g_ref[...]).astype(BF16)
        h_ref[...] = h
        f_ref[...] = jnp.dot(h, wf_ref[...], preferred_element_type=F32) + bf_ref[...]

    o_ref[...] = jnp.dot(h_ref[...], w_ref[...], preferred_element_type=F32).astype(o_ref.dtype)


def _in_proj(x2, g, w, w_f, layer, b_f, *, tm=1024, tn=1536):
    m, d = x2.shape
    n = w.shape[2]
    tm, tn = _tile(m, tm), _tile(n, tn)
    return pl.pallas_call(
        _in_proj_kernel,
        grid=(m // tm, n // tn),
        in_specs=[pl.BlockSpec((tm, d), lambda i, j: (i, 0)),
                  pl.BlockSpec((1, d), lambda i, j: (0, 0)),
                  pl.BlockSpec((None, d, tn), lambda i, j: (layer, 0, j)),
                  pl.BlockSpec((None, d, LANES), lambda i, j: (layer, 0, 0)),
                  pl.BlockSpec((1, LANES), lambda i, j: (0, 0))],
        out_specs=[pl.BlockSpec((tm, tn), lambda i, j: (i, j)),
                   pl.BlockSpec((tm, LANES), lambda i, j: (i, 0)),
                   pl.BlockSpec((tm, d), lambda i, j: (i, 0))],
        out_shape=[jax.ShapeDtypeStruct((m, n), BF16), jax.ShapeDtypeStruct((m, LANES), F32),
                   jax.ShapeDtypeStruct((m, d), BF16)],
        compiler_params=_params("parallel", "arbitrary"),
        name="in_proj",
    )(x2, g, w, w_f, b_f)


def _cumsum_kernel(f_ref, o_ref, *, chunk):
    s = f_ref.shape[0]
    row = lax.broadcasted_iota(jnp.int32, (chunk, chunk), 0)
    col = lax.broadcasted_iota(jnp.int32, (chunk, chunk), 1)
    tril = (row >= col).astype(F32)
    carry = jnp.zeros((1, f_ref.shape[1]), F32)
    for c in range(s // chunk):
        ls = _log_sigmoid(f_ref[c * chunk:(c + 1) * chunk, :])
        cs = jnp.dot(tril, ls, preferred_element_type=F32, precision=lax.Precision.HIGHEST) + carry
        o_ref[c * chunk:(c + 1) * chunk, :] = cs
        carry = cs[chunk - 1:chunk, :]


def _log_forget_cumsum(f3):
    b, s, n = f3.shape
    chunk = _tile(s, LANES)
    return pl.pallas_call(
        functools.partial(_cumsum_kernel, chunk=chunk),
        grid=(b,),
        in_specs=[pl.BlockSpec((None, s, n), lambda i: (i, 0, 0))],
        out_specs=pl.BlockSpec((None, s, n), lambda i: (i, 0, 0)),
        out_shape=jax.ShapeDtypeStruct((b, s, n), F32),
        compiler_params=_params("parallel"),
        name="log_forget_cumsum",
    )(f3)


def _shift_rows(win, p):
    if p == 0:
        return win
    return pltpu.roll(win, win.shape[0] - p, axis=0)


def _mixers_kernel(z_ref, cw_ref, cb_ref, lg_ref, lb_ref, pw_ref, ps_ref, o_ref, zp_ref, up_ref, cv_ref, *, rows):
    s = z_ref.shape[0]
    n_chunks = s // rows
    zp_ref[0:HALO, :] = jnp.zeros((HALO, CONV_WIDTH), F32)
    up_ref[0:HALO, :] = jnp.zeros((HALO, POOL_WIDTH), F32)

    def fill(c, _):
        r0 = pl.multiple_of(c * rows, rows)
        a = z_ref[pl.ds(r0, rows), 0:CONV_WIDTH].astype(F32)
        gt = z_ref[pl.ds(r0, rows), CONV_WIDTH:OFF_B].astype(F32)
        zp_ref[pl.ds(r0 + HALO, rows), :] = a * _sigmoid(gt)
        up_ref[pl.ds(r0 + HALO, rows), :] = z_ref[pl.ds(r0, rows), OFF_B:OFF_Q].astype(F32)
        return 0

    lax.fori_loop(0, n_chunks, fill, 0)

    lead = HALO - (CONV_K - 1)

    def mix(c, _):
        r0 = pl.multiple_of(c * rows, rows)
        for lg in range(CONV_WIDTH // LANES):
            ls = slice(lg * LANES, (lg + 1) * LANES)
            win = zp_ref[pl.ds(r0, rows + HALO), ls]
            acc = jnp.zeros((rows, LANES), F32) + cb_ref[:, ls]
            for p in range(SUBLANES):
                wp = _shift_rows(win, p)
                for k in range(CONV_K):
                    if (lead + k) % SUBLANES == p:
                        off = lead + k - p
                        acc = acc + cw_ref[k:k + 1, ls] * wp[off:off + rows, :]
            cv_ref[:, ls] = acc
        y = cv_ref[...]
        mu = jnp.mean(y, axis=-1, keepdims=True)
        yc = y - mu
        var = jnp.mean(yc * yc, axis=-1, keepdims=True)
        y = yc * lax.rsqrt(var + EPS) * lg_ref[...] + lb_ref[...]
        o_ref[pl.ds(r0, rows), 0:CONV_WIDTH] = (y * _sigmoid(y)).astype(o_ref.dtype)

        half = HALO // 2
        pos = (lax.broadcasted_iota(jnp.int32, (rows, LANES), 0) + (r0 + 1)).astype(F32)
        for gi, w in enumerate(POOL_WINDOWS):
            ls = slice(gi * POOL_GROUP, (gi + 1) * POOL_GROUP)
            win = up_ref[pl.ds(r0 + half, rows + half), ls]
            tot = win
            span = 1
            while span < w:
                tot = tot + pltpu.roll(tot, span, axis=0)
                span *= 2
            u = win[half:half + rows, :]
            p = tot[half:half + rows, :] / jnp.minimum(pos, float(w)) - u
            yb = jnp.dot(p.astype(BF16), pw_ref[gi], preferred_element_type=F32) * ps_ref[:, ls]
            o_ref[pl.ds(r0, rows), CONV_WIDTH + gi * POOL_GROUP:CONV_WIDTH + (gi + 1) * POOL_GROUP] = yb.astype(o_ref.dtype)
        return 0

    lax.fori_loop(0, n_chunks, mix, 0)


def _mixers(z3, conv_w, conv_b, cln_g, cln_b, pool_w, pool_scale):
    b, s, n = z3.shape
    rows = _tile(s, 256)
    width = OFF_Q
    assert n % width == 0
    return pl.pallas_call(
        functools.partial(_mixers_kernel, rows=rows),
        grid=(b,),
        in_specs=[pl.BlockSpec((None, s, width), lambda i: (i, 0, 0)),
                  pl.BlockSpec(conv_w.shape, lambda i: (0, 0)),
                  pl.BlockSpec(conv_b.shape, lambda i: (0, 0)),
                  pl.BlockSpec(cln_g.shape, lambda i: (0, 0)),
                  pl.BlockSpec(cln_b.shape, lambda i: (0, 0)),
                  pl.BlockSpec(pool_w.shape, lambda i: (0, 0, 0)),
                  pl.BlockSpec(pool_scale.shape, lambda i: (0, 0))],
        out_specs=pl.BlockSpec((None, s, CONV_WIDTH + POOL_WIDTH), lambda i: (i, 0, 0)),
        out_shape=jax.ShapeDtypeStruct((b, s, CONV_WIDTH + POOL_WIDTH), BF16),
        scratch_shapes=[pltpu.VMEM((s + HALO, CONV_WIDTH), F32),
                        pltpu.VMEM((s + HALO, POOL_WIDTH), F32),
                        pltpu.VMEM((rows, CONV_WIDTH), F32)],
        compiler_params=_params("parallel"),
        name="conv_pool_mixers",
    )(z3, conv_w, conv_b, cln_g, cln_b, pool_w, pool_scale)


def _split3(c):
    hi = c.astype(BF16).astype(F32)
    r1 = c - hi
    mid = r1.astype(BF16).astype(F32)
    return hi, mid, r1 - mid


N_PIECES = 3


def _placement():
    e = np.zeros((ATT_HEADS, N_PIECES * LANES, 2 * LANES), np.float32)
    for h in range(ATT_HEADS):
        for p in range(N_PIECES):
            e[h, p * LANES + h, p] = 1.0
            e[h, p * LANES + h, LANES + N_PIECES + p] = -1.0
    return jnp.asarray(e, BF16)


def _extension_ones():
    o = np.zeros((1, 2 * LANES), np.float32)
    o[0, N_PIECES:2 * N_PIECES] = 1.0
    o[0, LANES:LANES + N_PIECES] = 1.0
    return jnp.asarray(o)


def _mean_sq_lanes(x, ones):
    hi, lo = _split2(x * x)
    tot = jnp.dot(hi, ones, preferred_element_type=F32) + jnp.dot(lo, ones, preferred_element_type=F32)
    return tot * (1.0 / HEAD_DIM)


def _qk_prep_kernel(qa_ref, qb_ref, ka_ref, kb_ref, c_ref, qg_ref, kg_ref, place_ref, ext1_ref, ones_ref,
                    qo_ref, ko_ref):
    scale = HEAD_DIM ** -0.5
    pieces = jnp.concatenate(_split3(c_ref[...]), axis=1).astype(BF16)
    ones = ones_ref[...]
    q_gain = qg_ref[...] * scale
    k_gain = kg_ref[...]
    half = ATT_HEADS // 2
    for h in range(ATT_HEADS):
        q_src, k_src = (qa_ref, ka_ref) if h < half else (qb_ref, kb_ref)
        ls = slice((h % half) * HEAD_DIM, (h % half + 1) * HEAD_DIM)
        q = q_src[:, ls].astype(F32)
        k = k_src[:, ls].astype(F32)
        qn = q * lax.rsqrt(_mean_sq_lanes(q, ones) + EPS) * q_gain
        kn = k * lax.rsqrt(_mean_sq_lanes(k, ones) + EPS) * k_gain
        ext = jnp.dot(pieces, place_ref[h], preferred_element_type=F32) + ext1_ref[...]
        qo_ref[h, :, 0:HEAD_DIM] = qn.astype(BF16)
        qo_ref[h, :, HEAD_DIM:2 * HEAD_DIM] = ext[:, 0:LANES].astype(BF16)
        ko_ref[h, :, 0:HEAD_DIM] = kn.astype(BF16)
        ko_ref[h, :, HEAD_DIM:2 * HEAD_DIM] = ext[:, LANES:2 * LANES].astype(BF16)


def _qk_prep(z3, c3, q_g, k_g):
    b, s, _ = z3.shape
    ts = _tile(s, 512)
    blk = (ATT_HEADS // 2) * HEAD_DIM
    qb0, kb0 = OFF_Q // blk, OFF_K // blk
    assert OFF_Q % blk == 0 and OFF_K % blk == 0
    out = jax.ShapeDtypeStruct((b, ATT_HEADS, s, 2 * HEAD_DIM), BF16)
    zspec = lambda cb: pl.BlockSpec((None, ts, blk), lambda i, j, cb=cb: (i, j, cb))
    ospec = pl.BlockSpec((None, ATT_HEADS, ts, 2 * HEAD_DIM), lambda i, j: (i, 0, j, 0))
    return pl.pallas_call(
        _qk_prep_kernel,
        grid=(b, s // ts),
        in_specs=[zspec(qb0), zspec(qb0 + 1), zspec(kb0), zspec(kb0 + 1),
                  pl.BlockSpec((None, ts, LANES), lambda i, j: (i, j, 0)),
                  pl.BlockSpec((1, HEAD_DIM), lambda i, j: (0, 0)),
                  pl.BlockSpec((1, HEAD_DIM), lambda i, j: (0, 0)),
                  pl.BlockSpec((ATT_HEADS, N_PIECES * LANES, 2 * LANES), lambda i, j: (0, 0, 0)),
                  pl.BlockSpec((1, 2 * LANES), lambda i, j: (0, 0)),
                  pl.BlockSpec((HEAD_DIM, HEAD_DIM), lambda i, j: (0, 0))],
        out_specs=[ospec, ospec],
        out_shape=[out, out],
        compiler_params=_params("parallel", "parallel"),
        name="qk_prep",
    )(z3, z3, z3, z3, c3, q_g, k_g, _placement(), _extension_ones(), jnp.ones((HEAD_DIM, HEAD_DIM), BF16))


ATT_HEADS_PER_STEP = 2


def _attn_kernel(q_ref, k_ref, v_ref, o_ref, *, tq):
    heads, s_len, _ = q_ref.shape
    for hh in range(heads):
        ls = slice(hh * HEAD_DIM, (hh + 1) * HEAD_DIM)
        for i in range(s_len // tq):
            n = (i + 1) * tq
            q = q_ref[hh, i * tq:(i + 1) * tq, :]
            s = lax.dot_general(q, k_ref[hh, 0:n, :], (((1,), (1,)), ((), ())), preferred_element_type=F32)
            row = lax.broadcasted_iota(jnp.int32, s.shape, 0) + i * tq
            col = lax.broadcasted_iota(jnp.int32, s.shape, 1)
            s = jnp.where(col <= row, s, NEG_BIG)
            m = jnp.max(s, axis=-1, keepdims=True)
            p = jnp.exp(s - m)
            l = jnp.sum(p, axis=-1, keepdims=True)
            acc = jnp.dot(p.astype(BF16), v_ref[0:n, ls], preferred_element_type=F32)
            o_ref[i * tq:(i + 1) * tq, ls] = (acc / l).astype(o_ref.dtype)


def _attention(qp, kp, z3):
    b, h, s, dk = qp.shape
    tq = _tile(s, 512)
    hs = ATT_HEADS_PER_STEP
    width = hs * HEAD_DIM
    assert h % hs == 0 and OFF_V % width == 0
    v0 = OFF_V // width
    return pl.pallas_call(
        functools.partial(_attn_kernel, tq=tq),
        grid=(b, h // hs),
        in_specs=[pl.BlockSpec((None, hs, s, dk), lambda bi, hi: (bi, hi, 0, 0)),
                  pl.BlockSpec((None, hs, s, dk), lambda bi, hi: (bi, hi, 0, 0)),
                  pl.BlockSpec((None, s, width), lambda bi, hi: (bi, 0, v0 + hi))],
        out_specs=pl.BlockSpec((None, s, width), lambda bi, hi: (bi, 0, hi)),
        out_shape=jax.ShapeDtypeStruct((b, s, ATT_WIDTH), BF16),
        compiler_params=_params("parallel", "parallel"),
        name="fox_attention",
    )(qp, kp, z3)


def _merge_kernel(h_ref, ya_ref, yb_ref, yc_ref, wa_ref, wb_ref, wc_ref,
                  wg0_ref, wg1_ref, wg2_ref, bg0_ref, bg1_ref, bg2_ref, o_ref):
    h = h_ref[...]

    def gated(y_ref, w_ref, wg_ref, bg_ref):
        gate = _sigmoid(jnp.dot(h, wg_ref[...], preferred_element_type=F32) + bg_ref[...])
        return gate * jnp.dot(y_ref[...], w_ref[...], preferred_element_type=F32)

    merged = (gated(ya_ref, wa_ref, wg0_ref, bg0_ref) + gated(yb_ref, wb_ref, wg1_ref, bg1_ref)
              + gated(yc_ref, wc_ref, wg2_ref, bg2_ref))
    o_ref[...] = merged.astype(o_ref.dtype)


def _merge(h, y_ab, y_c, w_br, w_gate, layer, b_gate, *, tm=1024, tn=512):
    m, d = h.shape
    tm, tn = _tile(m, tm), _tile(d, tn)
    nj = d // tn
    gate_w = lambda br: pl.BlockSpec((None, d, tn), lambda i, j, br=br: (layer, 0, br * nj + j))
    gate_b = lambda br: pl.BlockSpec((1, tn), lambda i, j, br=br: (0, br * nj + j))
    return pl.pallas_call(
        _merge_kernel,
        grid=(m // tm, nj),
        in_specs=[pl.BlockSpec((tm, d), lambda i, j: (i, 0)),
                  pl.BlockSpec((tm, CONV_WIDTH), lambda i, j: (i, 0)),
                  pl.BlockSpec((tm, POOL_WIDTH), lambda i, j: (i, 1)),
                  pl.BlockSpec((tm, ATT_WIDTH), lambda i, j: (i, 0)),
                  pl.BlockSpec((None, CONV_WIDTH, tn), lambda i, j: (layer, 0, j)),
                  pl.BlockSpec((None, POOL_WIDTH, tn), lambda i, j: (layer, 1, j)),
                  pl.BlockSpec((None, ATT_WIDTH, tn), lambda i, j: (layer, 1, j)),
                  gate_w(0), gate_w(1), gate_w(2), gate_b(0), gate_b(1), gate_b(2)],
        out_specs=pl.BlockSpec((tm, tn), lambda i, j: (i, j)),
        out_shape=jax.ShapeDtypeStruct((m, d), BF16),
        compiler_params=_params("parallel", "arbitrary"),
        name="gated_branch_merge",
    )(h, y_ab, y_ab, y_c, w_br, w_br, w_br, w_gate, w_gate, w_gate, b_gate, b_gate, b_gate)


def _out_proj_kernel(a_ref, w_ref, x_ref, o_ref):
    o_ref[...] = x_ref[...] + jnp.dot(a_ref[...], w_ref[...], preferred_element_type=F32)


def _out_proj(a, w, layer, x2, *, tm=1024, tn=1024):
    m, k = a.shape
    n = w.shape[2]
    tm, tn = _tile(m, tm), _tile(n, tn)
    return pl.pallas_call(
        _out_proj_kernel,
        grid=(m // tm, n // tn),
        in_specs=[pl.BlockSpec((tm, k), lambda i, j: (i, 0)),
                  pl.BlockSpec((None, k, tn), lambda i, j: (layer, 0, j)),
                  pl.BlockSpec((tm, tn), lambda i, j: (i, j))],
        out_specs=pl.BlockSpec((tm, tn), lambda i, j: (i, j)),
        out_shape=jax.ShapeDtypeStruct((m, n), F32),
        compiler_params=_params("parallel", "arbitrary"),
        name="out_proj_residual",
    )(a, w, x2)


def _swiglu_partial(h, wg, wu, wd):
    a = jnp.dot(h, wg, preferred_element_type=F32)
    u = jnp.dot(h, wu, preferred_element_type=F32)
    act = (a * _sigmoid(a) * u).astype(BF16)
    return jnp.dot(act, wd, preferred_element_type=F32)


def _ffn_kernel(x_ref, g_ref, wg_ref, wu_ref, wd_ref, o_ref, h_ref):
    @pl.when(pl.program_id(1) == 0)
    def _():
        x = x_ref[...]
        h_ref[...] = _rmsnorm_rows(x, g_ref[...]).astype(BF16)
        o_ref[...] = x

    o_ref[...] += _swiglu_partial(h_ref[...], wg_ref[...], wu_ref[...], wd_ref[...])


def _ffn(x2, g, wg, wu, wd, layer, *, tm=1024, tf=512):
    m, d = x2.shape
    ff = wg.shape[2]
    tm, tf = _tile(m, tm), _tile(ff, tf)
    return pl.pallas_call(
        _ffn_kernel,
        grid=(m // tm, ff // tf),
        in_specs=[pl.BlockSpec((tm, d), lambda i, f: (i, 0)),
                  pl.BlockSpec((1, d), lambda i, f: (0, 0)),
                  pl.BlockSpec((None, d, tf), lambda i, f: (layer, 0, f)),
                  pl.BlockSpec((None, d, tf), lambda i, f: (layer, 0, f)),
                  pl.BlockSpec((None, tf, d), lambda i, f: (layer, f, 0))],
        out_specs=pl.BlockSpec((tm, d), lambda i, f: (i, 0)),
        out_shape=jax.ShapeDtypeStruct((m, d), F32),
        scratch_shapes=[pltpu.VMEM((tm, d), BF16)],
        compiler_params=_params("parallel", "arbitrary"),
        name="swiglu_dense",
    )(x2, g, wg, wu, wd)


def _moe_ffn_kernel(te_ref, nu_ref, tok_ref, x_ref, g_ref, wg_ref, wu_ref, wd_ref, o_ref,
                    xbuf_ref, h_ref, sem, *, tm, rows_step):
    i = pl.program_id(0)
    f = pl.program_id(1)
    n_used = nu_ref[0]
    used = i < n_used
    slot = i % 2
    rows_tile = xbuf_ref.shape[1]

    def row_copy(tile, r, slot_):
        return pltpu.make_async_copy(x_ref.at[pl.ds(tok_ref[tile * tm + r], 1)],
                                     xbuf_ref.at[slot_, pl.ds(r, 1)], sem.at[slot_])

    @pl.when((i == 0) & (f == 0))
    def _():
        def body(r, c):
            row_copy(0, r, 0).start()
            return c
        lax.fori_loop(0, rows_tile, body, 0, unroll=8)

    @pl.when((f == 0) & (i <= n_used))
    def _():
        def body(r, c):
            row_copy(i, r, slot).wait()
            return c
        lax.fori_loop(0, rows_tile, body, 0, unroll=8)

    @pl.when(f == 0)
    def _():
        o_ref[...] = jnp.zeros(o_ref.shape, F32)

    @pl.when(used & (f == 0))
    def _():
        h_ref[...] = _rmsnorm_rows(xbuf_ref[slot, pl.ds(0, tm), :], g_ref[...]).astype(BF16)

    @pl.when(used)
    def _():
        for r in range(rows_step):
            row_copy(i + 1, f * rows_step + r, 1 - slot).start()
        o_ref[...] += _swiglu_partial(h_ref[...], wg_ref[...], wu_ref[...], wd_ref[...])


def _moe_ffn(tile_expert, n_used, src_tok, x2, g, wg, wu, wd, layer, *, tm, tf=512):
    mp = src_tok.shape[0]
    d = x2.shape[1]
    ff = wg.shape[3]
    tf = _tile(ff, tf)
    nf = ff // tf
    rows_step = -(-tm // (nf * SUBLANES)) * SUBLANES
    rows_tile = rows_step * nf
    src_tok = jnp.pad(src_tok, (0, 2 * rows_tile))

    def wspec(shape, fdim):
        def imap(i, f, te, nu, tok):
            fi = jnp.where(i < nu[0], f, nf - 1)
            return (layer, te[i], fi, 0) if fdim == 1 else (layer, te[i], 0, fi)
        return pl.BlockSpec(shape, imap)

    return pl.pallas_call(
        functools.partial(_moe_ffn_kernel, tm=tm, rows_step=rows_step),
        grid_spec=pltpu.PrefetchScalarGridSpec(
            num_scalar_prefetch=3,
            grid=(mp // tm, nf),
            in_specs=[pl.BlockSpec(memory_space=pl.ANY),
                      pl.BlockSpec((1, d), lambda i, f, te, nu, tok: (0, 0)),
                      wspec((None, None, d, tf), 2),
                      wspec((None, None, d, tf), 2),
                      wspec((None, None, tf, d), 1)],
            out_specs=pl.BlockSpec((tm, d), lambda i, f, te, nu, tok: (i, 0)),
            scratch_shapes=[pltpu.VMEM((2, rows_tile, d), F32), pltpu.VMEM((tm, d), BF16),
                            pltpu.SemaphoreType.DMA((2,))]),
        out_shape=jax.ShapeDtypeStruct((mp, d), F32),
        compiler_params=_params("arbitrary", "arbitrary"),
        name="swiglu_experts_grouped",
    )(tile_expert, n_used, src_tok, x2, g, wg, wu, wd)


def _combine_kernel(p1_ref, p2_ref, x_ref, sel_ref, y_ref, o_ref, buf_ref, sem, *, tc):
    base = pl.program_id(0) * tc

    def row_copy(src_row, slot, r):
        return pltpu.make_async_copy(y_ref.at[pl.ds(src_row, 1)], buf_ref.at[slot, pl.ds(r, 1)], sem)

    def issue(grp, c):
        r0 = pl.multiple_of(grp * SUBLANES, SUBLANES)
        for k in range(SUBLANES):
            row_copy(p1_ref[base + r0 + k], 0, r0 + k).start()
            row_copy(p2_ref[base + r0 + k], 1, r0 + k).start()
        return c

    def drain(r, c):
        row_copy(0, 0, r).wait()
        row_copy(0, 1, r).wait()
        return c

    lax.fori_loop(0, tc // SUBLANES, issue, 0)
    lax.fori_loop(0, tc, drain, 0, unroll=8)
    sel = sel_ref[...]
    w1, w2 = sel[:, TOP_K:TOP_K + 1], sel[:, TOP_K + 1:TOP_K + 2]
    o_ref[...] = x_ref[...] + w1 * buf_ref[0] + w2 * buf_ref[1]


def _combine(p1, p2, x2, sel, y, *, tc=512):
    m, d = x2.shape
    tc = _tile(m, tc)
    return pl.pallas_call(
        functools.partial(_combine_kernel, tc=tc),
        grid_spec=pltpu.PrefetchScalarGridSpec(
            num_scalar_prefetch=2,
            grid=(m // tc,),
            in_specs=[pl.BlockSpec((tc, d), lambda i, p1, p2: (i, 0)),
                      pl.BlockSpec((tc, LANES), lambda i, p1, p2: (i, 0)),
                      pl.BlockSpec(memory_space=pl.ANY)],
            out_specs=pl.BlockSpec((tc, d), lambda i, p1, p2: (i, 0)),
            scratch_shapes=[pltpu.VMEM((TOP_K, tc, d), F32), pltpu.SemaphoreType.DMA(())]),
        out_shape=jax.ShapeDtypeStruct((m, d), F32),
        compiler_params=_params("arbitrary"),
        name="combine_expert_rows",
    )(p1, p2, x2, sel, y)


def _route(sel, tm):
    m = sel.shape[0]
    e_flat = sel[:, :TOP_K].astype(jnp.int32).T.reshape(-1)
    onehot = (e_flat[:, None] == jnp.arange(N_EXPERTS, dtype=jnp.int32)[None, :]).astype(jnp.int32)
    csum = jnp.cumsum(onehot, axis=0)
    rank = jnp.sum((csum - onehot) * onehot, axis=1)
    counts = csum[-1]
    padded = ((counts + tm - 1) // tm) * tm
    ends = jnp.cumsum(padded)
    dest = (ends - padded)[e_flat] + rank
    mp = TOP_K * m + N_EXPERTS * tm
    n_tiles = mp // tm
    tok = jnp.tile(jnp.arange(m, dtype=jnp.int32), TOP_K)
    src_tok = jnp.zeros((mp,), jnp.int32).at[dest].set(tok)
    n_used = (ends[-1] // tm).astype(jnp.int32)
    tile_id = jnp.arange(n_tiles, dtype=jnp.int32)
    tile_e = jnp.sum((ends[None, :] <= (tile_id * tm)[:, None]).astype(jnp.int32), axis=1)
    tile_e = jnp.minimum(tile_e, N_EXPERTS - 1)
    tile_e = jnp.where(tile_id < n_used, tile_e, tile_e[jnp.maximum(n_used - 1, 0)])
    return src_tok, tile_e, n_used.reshape(1), dest[:m], dest[m:]


def _moe(x2, g, router_w, wg, wu, wd, layer):
    m, d = x2.shape
    tm = _tile(m, MOE_ROW_TILE)
    sel = _router(x2, g, router_w)
    src_tok, tile_e, n_used, p1, p2 = _route(sel, tm)
    y = _moe_ffn(tile_e, n_used, src_tok, x2, g, wg, wu, wd, layer, tm=tm)
    return _combine(p1, p2, x2, sel, y)


def _pad_cols(a, n):
    return jnp.pad(a, ((0, 0), (0, n - a.shape[1])))


def _mixing_sublayer(x2, bsz, seq, layer, norm_g, w_main, w_f, w_gate, b_f, b_gate, conv_w, conv_b, cln_g, cln_b,
                     pool_w, pool_scale, q_g, k_g, w_br, w_o):
    row = lambda v: v.reshape(1, -1)
    z, f_logit, h = _in_proj(x2, row(norm_g), w_main, w_f, layer, _pad_cols(row(b_f), LANES))

    z3 = z.reshape(bsz, seq, OFF_F)
    c3 = _log_forget_cumsum(f_logit.reshape(bsz, seq, LANES))
    y_ab = _mixers(z3, conv_w, row(conv_b), row(cln_g), row(cln_b), pool_w.astype(BF16), row(pool_scale))
    qp, kp = _qk_prep(z3, c3, row(q_g), row(k_g))
    y_c = _attention(qp, kp, z3)

    merged = _merge(h, y_ab.reshape(bsz * seq, -1), y_c.reshape(bsz * seq, -1), w_br, w_gate, layer, row(b_gate))
    return _out_proj(merged, w_o, layer, x2)


def kernel(x, norm1_g, w_in, b_f, b_gate, conv_w, conv_b, cln_g, cln_b, pool_w, pool_scale, q_g, k_g, w_br, w_o,
           norm2_g, ffn_wg, ffn_wu, ffn_wd, router, exp_wg, exp_wu, exp_wd):
    bsz, seq, d = x.shape
    depth = w_in.shape[0]
    x2 = x.reshape(bsz * seq, d)
    w_main = w_in[:, :, :OFF_F].astype(BF16)
    w_gate = w_in[:, :, OFF_G:].astype(BF16)
    w_f = jnp.pad(w_in[:, :, OFF_F:OFF_G], ((0, 0), (0, 0), (0, LANES - ATT_HEADS))).astype(BF16)
    w_br16, w_o16 = w_br.astype(BF16), w_o.astype(BF16)
    ffn16 = [w.astype(BF16) for w in (ffn_wg, ffn_wu, ffn_wd)]
    exp16 = [w.astype(BF16) for w in (exp_wg, exp_wu, exp_wd)]
    for l in range(depth):
        x2 = _mixing_sublayer(x2, bsz, seq, l, norm1_g[l], w_main, w_f, w_gate, b_f[l], b_gate[l],
                              conv_w[l], conv_b[l], cln_g[l], cln_b[l], pool_w[l], pool_scale[l],
                              q_g[l], k_g[l], w_br16, w_o16)
        g2 = norm2_g[l].reshape(1, d)
        j = l // 2
        if l % 2 == 0:
            x2 = _ffn(x2, g2, *ffn16, j)
        else:
            x2 = _moe(x2, g2, router[j], *exp16, j)
    return x2.reshape(bsz, seq, d)
```
